```python
import math
import jax, jax.numpy as jnp
from jax import lax
import numpy as np

D_MODEL = 1024
BATCH = 8
SEQ = 2048
DEPTH = 4
DEC_BATCH = 128
DEC_SEQ = 4
PAST_LEN = 16384
PAGE_SIZE = 128

N_EVEN = (DEPTH + 1) // 2
N_ODD = DEPTH // 2
N_VMIX = max(N_ODD - 1, 0)
MIX_A = D_MODEL // 2
H_A = 4
DV_A = MIX_A // H_A
DK_A = DV_A // 2
KA_DIM = H_A * DK_A
GK_RANK = 16
GK_NORM = 16.0
MIX_B = D_MODEL - MIX_A
H_B = 4
DV_B = MIX_B // H_B
N_B = 128
KB_DIM = H_B * N_B
SPLIT_SIZES = (KA_DIM, KA_DIM, MIX_A, MIX_A, GK_RANK, KB_DIM, KB_DIM, MIX_B, MIX_B)
IN_EVEN = sum(SPLIT_SIZES)
CHUNK = 64
LB_FLOOR = 1e-20
N_C = 64
H_C = D_MODEL // N_C
D_DECAY_LORA = 64
D_AAA_LORA = 64
D_MV_LORA = 32
D_GATE_LORA = 128
FFN_DIM = -(-8 * D_MODEL // (3 * 256)) * 256
PLE_DIM = 256
RMS_EPS = 1e-6
GN_EPS = 64e-5

kernel_name = 'gla_hgrn2_rwkv7_hybrid_step'


def rmsnorm(x, g, eps=RMS_EPS):
    xf = x.astype(jnp.float32)
    y = xf * lax.rsqrt(jnp.mean(xf * xf, axis=-1, keepdims=True) + eps)
    return (y * g.astype(jnp.float32)).astype(x.dtype)


def chunk_gated_linear(q, k, v, log_g, s0):
    f32 = jnp.float32
    bsz, t_len, n_h, _ = q.shape
    d_v = v.shape[-1]
    c = math.gcd(t_len, CHUNK)
    n_chunks = t_len // c

    def to_chunks(z):
        return z.astype(f32).reshape(bsz, n_chunks, c, n_h, z.shape[-1]).transpose(1, 0, 3, 2, 4)

    causal = jnp.tril(jnp.ones((c, c), dtype=bool))[:, :, None]

    def step(s, inp):
        qi, ki, vi, gi = inp
        b = jnp.cumsum(gi, axis=2)
        diff = b[:, :, :, None, :] - b[:, :, None, :, :]
        decay = jnp.where(causal, jnp.exp(jnp.minimum(diff, 0.0)), 0.0)
        scores = jnp.einsum('bhtc,bhsc,bhtsc->bhts', qi, ki, decay)
        o = (jnp.einsum('bhts,bhsv->bhtv', scores, vi)
             + jnp.einsum('bhtc,bhcv->bhtv', qi * jnp.exp(b), s))
        b_last = b[:, :, -1:, :]
        s_new = (jnp.exp(b_last[:, :, 0, :])[..., None] * s
                 + jnp.einsum('bhsc,bhsv->bhcv', ki * jnp.exp(b_last - b), vi))
        return s_new, o

    s_fin, o = lax.scan(step, s0.astype(f32),
                        (to_chunks(q), to_chunks(k), to_chunks(v), to_chunks(log_g)))
    o = o.transpose(1, 0, 3, 2, 4).reshape(bsz, t_len, n_h, d_v)
    return o, s_fin


def rwkv7_recurrence(r, log_w, k, v, kk, a, s0):
    f32 = jnp.float32

    def to_time(z):
        return jnp.moveaxis(z.astype(f32), 1, 0)

    def step(s, inp):
        r_t, lw_t, k_t, v_t, kk_t, a_t = inp
        s_kk = jnp.einsum('bhik,bhk->bhi', s, kk_t)
        s = (s * jnp.exp(lw_t)[:, :, None, :]
             - s_kk[..., :, None] * (kk_t * a_t)[..., None, :]
             + v_t[..., :, None] * k_t[..., None, :])
        return s, jnp.einsum('bhij,bhj->bhi', s, r_t)

    s_fin, o = lax.scan(step, s0.astype(f32),
                        (to_time(r), to_time(log_w), to_time(k), to_time(v), to_time(kk), to_time(a)))
    return jnp.moveaxis(o, 0, 1), s_fin


def gla_hgrn_mixer(xn, s_gla, s_hgrn, lb, w_in, w_gk2, b_gk, gla_norm, hgrn_norm, w_out):
    f32 = jnp.float32
    bsz, t_len, _ = xn.shape
    u = xn @ w_in
    cuts = [int(c) for c in np.cumsum(SPLIT_SIZES)[:-1]]
    q_a, k_a, v_a, g_a, gk_lr, q_b, f_b, i_b, g_b = jnp.split(u, cuts, axis=-1)

    def heads(z, n_h):
        return z.reshape(bsz, t_len, n_h, -1)

    log_alpha = jax.nn.log_sigmoid((gk_lr @ w_gk2 + b_gk).astype(f32)) / GK_NORM
    o_a, s_gla_new = chunk_gated_linear(heads(q_a, H_A) * DK_A ** -0.5, heads(k_a, H_A),
                                        heads(v_a, H_A), heads(log_alpha, H_A), s_gla)
    o_a = rmsnorm(o_a, gla_norm).reshape(bsz, t_len, MIX_A).astype(xn.dtype) * jax.nn.silu(g_a)
    z = f_b.astype(f32)
    log_f = jnp.logaddexp(jnp.log(jnp.maximum(lb, LB_FLOOR)), jnp.log1p(-lb) + jax.nn.log_sigmoid(z))
    k_b = (1.0 - lb) * jax.nn.sigmoid(-z)
    o_b, s_hgrn_new = chunk_gated_linear(heads(jax.nn.silu(q_b), H_B) * N_B ** -0.5, heads(k_b, H_B),
                                         heads(i_b, H_B), heads(log_f, H_B), s_hgrn)
    o_b = rmsnorm(o_b, hgrn_norm).reshape(bsz, t_len, MIX_B).astype(xn.dtype) * jax.nn.silu(g_b)
    return jnp.concatenate([o_a, o_b], axis=-1) @ w_out, s_gla_new, s_hgrn_new


def rwkv7_mixer(xn, s_wkv, x_last, v_first, mu, w_r, w_k, w_v, w_o, w0, w1, w2,
                a0, a1, a2, vmix, g1, g2, k_k, k_a, r_k, ln_w, ln_b):
    f32 = jnp.float32
    bsz, t_len, _ = xn.shape
    x_prev = jnp.concatenate([x_last[:, None, :].astype(xn.dtype), xn[:, :-1]], axis=1)
    xmix = xn[:, :, None, :] + (x_prev - xn)[:, :, None, :] * mu
    xr, xw, xk, xv, xa, xg = [xmix[:, :, m] for m in range(6)]
    r = xr @ w_r
    k = xk @ w_k
    v = xv @ w_v
    log_w = -jnp.exp(-jax.nn.softplus(-(w0 + jnp.tanh(xw @ w1) @ w2).astype(f32)) - 0.5)
    a = jax.nn.sigmoid((a0 + (xa @ a1) @ a2).astype(f32))
    g = jax.nn.sigmoid(xg @ g1) @ g2
    kk = (k * k_k).astype(f32).reshape(bsz, t_len, H_C, N_C)
    kk = kk * lax.rsqrt(jnp.maximum(jnp.sum(kk * kk, axis=-1, keepdims=True), 1e-24))
    k = k.astype(f32) * (1.0 + (a - 1.0) * k_a)
    if vmix is None:
        v_first = v
    else:
        v0, v1, v2 = vmix
        v = v + (v_first - v) * jax.nn.sigmoid(v0 + (xv @ v1) @ v2)

    def heads(z):
        return z.astype(f32).reshape(bsz, t_len, H_C, N_C)

    o, s_new = rwkv7_recurrence(heads(r), heads(log_w), heads(k), heads(v), kk, heads(a), s_wkv)
    mean = jnp.mean(o, axis=-1, keepdims=True)
    var = jnp.mean(jnp.square(o - mean), axis=-1, keepdims=True)
    on = ((o - mean) * lax.rsqrt(var + GN_EPS)).reshape(bsz, t_len, D_MODEL) * ln_w + ln_b
    bonus = jnp.sum(heads(r) * heads(k) * r_k, axis=-1, keepdims=True) * heads(v)
    on = on + bonus.reshape(bsz, t_len, D_MODEL)
    return (on.astype(xn.dtype) * g) @ w_o, s_new, xn[:, -1], v_first


def swiglu(x, w_gate, w_up, w_down):
    return (jax.nn.silu(x @ w_gate) * (x @ w_up)) @ w_down


def setup_inputs(seed: int = 0) -> dict:
    key = jax.random.key(seed)
    ks = iter(jax.random.split(key, 64))

    def nrm(shape, scale):
        return scale * jax.random.normal(next(ks), shape, jnp.float32)

    def uni(shape, lo, hi):
        return jax.random.uniform(next(ks), shape, jnp.float32, lo, hi)

    sd = D_MODEL ** -0.5
    return {
        'x_prompt': nrm((BATCH, SEQ, D_MODEL), 1.0),
        'x_sample': nrm((DEC_BATCH, DEC_SEQ, D_MODEL), 1.0),
        'p_prompt': nrm((DEPTH, BATCH, SEQ, PLE_DIM), 1.0),
        'p_sample': nrm((DEPTH, DEC_BATCH, DEC_SEQ, PLE_DIM), 1.0),
        'state_gla': nrm((N_EVEN, DEC_BATCH, H_A, DK_A, DV_A), 0.2),
        'state_hgrn': nrm((N_EVEN, DEC_BATCH, H_B, N_B, DV_B), 0.2),
        'state_rwkv': nrm((N_ODD, DEC_BATCH, H_C, N_C, N_C), 0.1),
        'state_shift': nrm((N_ODD, DEC_BATCH, D_MODEL), 1.0),
        'norm_mix': 1.0 + nrm((DEPTH, D_MODEL), 0.05),
        'norm_ffn': 1.0 + nrm((DEPTH, D_MODEL), 0.05),
        'norm_ple': 1.0 + nrm((DEPTH, D_MODEL), 0.05),
        'norm_final': 1.0 + nrm((D_MODEL,), 0.05),
        'w_in_even': nrm((N_EVEN, D_MODEL, IN_EVEN), sd),
        'w_gk2': nrm((N_EVEN, GK_RANK, KA_DIM), GK_RANK ** -0.5),
        'b_gk': nrm((N_EVEN, KA_DIM), 0.1),
        'gla_norm': 1.0 + nrm((N_EVEN, DV_A), 0.05),
        'hgrn_gamma': nrm((N_EVEN, KB_DIM), 1.0),
        'hgrn_norm': 1.0 + nrm((N_EVEN, DV_B), 0.05),
        'w_out_even': nrm((N_EVEN, MIX_A + MIX_B, D_MODEL), sd),
        'rw_mu': uni((N_ODD, 6, D_MODEL), 0.0, 1.0),
        'rw_wr': nrm((N_ODD, D_MODEL, D_MODEL), sd),
        'rw_wk': nrm((N_ODD, D_MODEL, D_MODEL), sd),
        'rw_wv': nrm((N_ODD, D_MODEL, D_MODEL), sd),
        'rw_wo': nrm((N_ODD, D_MODEL, D_MODEL), sd),
        'rw_w0': uni((N_ODD, D_MODEL), -3.0, -0.5),
        'rw_w1': nrm((N_ODD, D_MODEL, D_DECAY_LORA), sd),
        'rw_w2': nrm((N_ODD, D_DECAY_LORA, D_MODEL), 0.1 * D_DECAY_LORA ** -0.5),
        'rw_a0': nrm((N_ODD, D_MODEL), 0.1),
        'rw_a1': nrm((N_ODD, D_MODEL, D_AAA_LORA), sd),
        'rw_a2': nrm((N_ODD, D_AAA_LORA, D_MODEL), 0.1 * D_AAA_LORA ** -0.5),
        'rw_v0': 0.5 + nrm((N_VMIX, D_MODEL), 0.1),
        'rw_v1': nrm((N_VMIX, D_MODEL, D_MV_LORA), sd),
        'rw_v2': nrm((N_VMIX, D_MV_LORA, D_MODEL), 0.1 * D_MV_LORA ** -0.5),
        'rw_g1': nrm((N_ODD, D_MODEL, D_GATE_LORA), sd),
        'rw_g2': nrm((N_ODD, D_GATE_LORA, D_MODEL), D_GATE_LORA ** -0.5),
        'rw_kk': 0.85 + nrm((N_ODD, D_MODEL), 0.05),
        'rw_ka': 1.0 + nrm((N_ODD, D_MODEL), 0.05),
        'rw_rk': nrm((N_ODD, H_C, N_C), 0.1),
        'rw_lnw': 1.0 + nrm((N_ODD, D_MODEL), 0.05),
        'rw_lnb': nrm((N_ODD, D_MODEL), 0.01),
        'w_ffn_gate': nrm((DEPTH, D_MODEL, FFN_DIM), sd),
        'w_ffn_up': nrm((DEPTH, D_MODEL, FFN_DIM), sd),
        'w_ffn_down': nrm((DEPTH, FFN_DIM, D_MODEL), FFN_DIM ** -0.5),
        'w_ple': nrm((DEPTH, PLE_DIM, D_MODEL), PLE_DIM ** -0.5),
        'w_ple_gate': nrm((DEPTH, D_MODEL, D_MODEL), sd),
    }


def reference(x_prompt, x_sample, p_prompt, p_sample, state_gla, state_hgrn, state_rwkv, state_shift,
              norm_mix, norm_ffn, norm_ple, norm_final,
              w_in_even, w_gk2, b_gk, gla_norm, hgrn_gamma, hgrn_norm, w_out_even,
              rw_mu, rw_wr, rw_wk, rw_wv, rw_wo, rw_w0, rw_w1, rw_w2, rw_a0, rw_a1, rw_a2,
              rw_v0, rw_v1, rw_v2, rw_g1, rw_g2, rw_kk, rw_ka, rw_rk, rw_lnw, rw_lnb,
              w_ffn_gate, w_ffn_up, w_ffn_down, w_ple, w_ple_gate):
    f32 = jnp.float32
    sm = jax.nn.softmax(hgrn_gamma.astype(f32), axis=0)
    lower_bounds = jnp.cumsum(sm, axis=0) - sm[0:1]

    def run(x, p, s_gla, s_hgrn, s_rwkv, s_shift):
        h = x
        out_gla, out_hgrn, out_rwkv, out_shift = [], [], [], []
        v_first = None
        for i in range(DEPTH):
            j = i // 2
            xn = rmsnorm(h, norm_mix[i])
            if i % 2 == 0:
                mix, sg, sh = gla_hgrn_mixer(xn, s_gla[j], s_hgrn[j], lower_bounds[j], w_in_even[j],
                                             w_gk2[j], b_gk[j], gla_norm[j], hgrn_norm[j], w_out_even[j])
                out_gla.append(sg)
                out_hgrn.append(sh)
            else:
                vmix = None if j == 0 else (rw_v0[j - 1], rw_v1[j - 1], rw_v2[j - 1])
                mix, sr, ss, v_first = rwkv7_mixer(xn, s_rwkv[j], s_shift[j], v_first, rw_mu[j], rw_wr[j],
                                                   rw_wk[j], rw_wv[j], rw_wo[j], rw_w0[j], rw_w1[j], rw_w2[j],
                                                   rw_a0[j], rw_a1[j], rw_a2[j], vmix, rw_g1[j], rw_g2[j],
                                                   rw_kk[j], rw_ka[j], rw_rk[j], rw_lnw[j], rw_lnb[j])
                out_rwkv.append(sr)
                out_shift.append(ss)
            h = h + mix
            h = h + swiglu(rmsnorm(h, norm_ffn[i]), w_ffn_gate[i], w_ffn_up[i], w_ffn_down[i])
            gate = jax.nn.sigmoid(rmsnorm(h, norm_ple[i]) @ w_ple_gate[i])
            h = h + gate * (p[i] @ w_ple[i])
        return (rmsnorm(h, norm_final), jnp.stack(out_gla), jnp.stack(out_hgrn),
                jnp.stack(out_rwkv), jnp.stack(out_shift))

    bp = x_prompt.shape[0]
    y_prompt, gla_p, hgrn_p, rwkv_p, shift_p = run(
        x_prompt, p_prompt,
        jnp.zeros((N_EVEN, bp, H_A, DK_A, DV_A), f32),
        jnp.zeros((N_EVEN, bp, H_B, N_B, DV_B), f32),
        jnp.zeros((N_ODD, bp, H_C, N_C, N_C), f32),
        jnp.zeros((N_ODD, bp, D_MODEL), x_prompt.dtype))
    y_sample, gla_s, hgrn_s, rwkv_s, shift_s = run(
        x_sample, p_sample, state_gla, state_hgrn, state_rwkv, state_shift)
    return (y_prompt, y_sample, gla_p, hgrn_p, rwkv_p, shift_p, gla_s, hgrn_s, rwkv_s, shift_s)
```

```python
import functools
import math

import jax
import jax.numpy as jnp
from jax import lax
from jax.experimental import pallas as pl
from jax.experimental.pallas import tpu as pltpu

F32 = jnp.float32
BF16 = jnp.bfloat16

D_MODEL = 1024
DEPTH = 4
MIX_A = 512
H_A = 4
DV_A = 128
DK_A = 64
KA_DIM = H_A * DK_A
GK_RANK = 16
GK_NORM = 16.0
MIX_B = 512
H_B = 4
DV_B = 128
N_B = 128
KB_DIM = H_B * N_B
LB_FLOOR = 1e-20
N_C = 64
H_C = D_MODEL // N_C
FFN_DIM = 2816
PLE_DIM = 256
RMS_EPS = 1e-6
GN_EPS = 64e-5
KK_EPS = 1e-24

LANE = 128
CHUNK = 64
SUB = 16
VMEM_LIMIT = 56 * 1024 * 1024
FFN_SPLIT = 1408


def _cparams(sem):
    return pltpu.CompilerParams(dimension_semantics=sem, vmem_limit_bytes=VMEM_LIMIT)


def _const_spec(shape):
    nd = len(shape)
    return pl.BlockSpec(shape, lambda *_: (0,) * nd, pipeline_mode=pl.Buffered(1))


def _rms(x, g, eps=RMS_EPS):
    return x * lax.rsqrt(jnp.mean(x * x, axis=-1, keepdims=True) + eps) * g


def _sigmoid(x):
    return 1.0 / (1.0 + jnp.exp(-x))


def _silu(x):
    return x * _sigmoid(x)


def _log_sigmoid(x):
    return jnp.minimum(x, 0.0) - jnp.log1p(jnp.exp(-jnp.abs(x)))


def _mm(a, b):
    return jnp.dot(a.astype(BF16), b.astype(BF16), preferred_element_type=F32)


def _mm_nt(a, b):
    return lax.dot_general(a.astype(BF16), b.astype(BF16), (((1,), (1,)), ((), ())),
                           preferred_element_type=F32)


def _mm_tn(a, b):
    return lax.dot_general(a.astype(BF16), b.astype(BF16), (((0,), (0,)), ((), ())),
                           preferred_element_type=F32)


def _mmh(a, b):
    return jnp.dot(a, b, precision=lax.Precision.HIGHEST, preferred_element_type=F32)


def _mmh_nt(a, b):
    return lax.dot_general(a, b, (((1,), (1,)), ((), ())), precision=lax.Precision.HIGHEST,
                           preferred_element_type=F32)


def _iota2(shape, axis):
    return lax.broadcasted_iota(jnp.int32, shape, axis)


def _cumsum_rows(x, tri):
    hi = x.astype(BF16)
    r1 = x - hi.astype(F32)
    mid = r1.astype(BF16)
    lo = (r1 - mid.astype(F32)).astype(BF16)
    d = lambda z: jnp.dot(tri, z, preferred_element_type=F32)
    return d(hi) + d(mid) + d(lo)


def _tri_ones(c):
    return (_iota2((c, c), 1) <= _iota2((c, c), 0)).astype(BF16)


def _even_pre_kernel(h_ref, gm_ref, w_ref, wgk1_ref, wgk2_ref, bgk_ref, gam_ref,
                     qa_ref, ka_ref, va_ref, ga_ref, lga_ref,
                     qb_ref, kb_ref, vb_ref, gb_ref, lgb_ref, *, layer):
    xn = _rms(h_ref[...], gm_ref[...]).astype(BF16)

    def proj(lo, hi):
        return jnp.dot(xn, w_ref[:, lo:hi], preferred_element_type=F32)

    qa_ref[...] = proj(0, 256) * DK_A ** -0.5
    ka_ref[...] = proj(256, 512)
    va_ref[...] = proj(512, 1024)
    ga_ref[...] = _silu(proj(1024, 1536))
    gk_lr = jnp.dot(xn, wgk1_ref[...], preferred_element_type=F32)
    gk = _mm(gk_lr, wgk2_ref[...]) + bgk_ref[...]
    lga_ref[...] = _log_sigmoid(gk) / GK_NORM

    gam = gam_ref[...]
    n_even = gam.shape[0]
    gmax = gam[0:1]
    for i in range(1, n_even):
        gmax = jnp.maximum(gmax, gam[i:i + 1])
    es = [jnp.exp(gam[i:i + 1] - gmax) for i in range(n_even)]
    den = es[0]
    for i in range(1, n_even):
        den = den + es[i]
    sm = [e / den for e in es]
    cum = sm[0]
    for i in range(1, layer + 1):
        cum = cum + sm[i]
    lb = cum - sm[0]

    qb_ref[...] = _silu(proj(1536, 2048)) * N_B ** -0.5
    z = proj(2048, 2560)
    lgb_ref[...] = jnp.log(jnp.maximum(lb, LB_FLOOR) + (1.0 - lb) * _sigmoid(z))
    kb_ref[...] = (1.0 - lb) * _sigmoid(-z)
    vb_ref[...] = proj(2560, 3072)
    gb_ref[...] = _silu(proj(3072, 3584))


def _even_pre(h, gm, w_cat, wgk1, wgk2, bgk, gamma, layer, tm):
    m = h.shape[0]
    widths = (256, 256, 512, 512, 256, 512, 512, 512, 512, 512)
    row = lambda w: pl.BlockSpec((tm, w), lambda i: (i, 0))
    return pl.pallas_call(
        functools.partial(_even_pre_kernel, layer=layer),
        out_shape=[jax.ShapeDtypeStruct((m, w), F32) for w in widths],
        grid=(m // tm,),
        in_specs=[row(D_MODEL), _const_spec(gm.shape), _const_spec(w_cat.shape),
                  _const_spec(wgk1.shape), _const_spec(wgk2.shape), _const_spec(bgk.shape),
                  _const_spec(gamma.shape)],
        out_specs=[row(w) for w in widths],
        compiler_params=_cparams(("parallel",)),
        name="even_pre",
    )(h, gm, w_cat, wgk1, wgk2, bgk, gamma)


def _gated_chunk(q_ref, k_ref, v_ref, lg_ref, col0, dk, vcol0, dv, s_ref, b_scr, k_scr, a_scr,
                 c, sb):
    ksl = slice(col0, col0 + dk)
    vsl = slice(vcol0, vcol0 + dv)
    q = q_ref[:, ksl]
    k = k_ref[:, ksl]
    v = v_ref[:, vsl]
    b = _cumsum_rows(lg_ref[:, ksl], _tri_ones(c))
    b_scr[:, 0:dk] = b
    k_scr[:, 0:dk] = k
    s0 = s_ref[...]
    o = _mm(q * jnp.exp(b), s0)

    lane = _iota2((sb, sb), 1)
    tril_sb = lane <= _iota2((sb, sb), 0)
    nsb = c // sb
    for blk in range(nsb):
        r0 = blk * sb
        q_i = q[r0:r0 + sb]
        b_i = b[r0:r0 + sb]

        def body(s, acc, r0=r0, q_i=q_i, b_i=b_i):
            b_s = b_scr[pl.ds(r0 + s, 1), 0:dk]
            k_s = k_scr[pl.ds(r0 + s, 1), 0:dk]
            e = jnp.exp(jnp.minimum(b_i - b_s, 0.0))
            col = jnp.sum(q_i * e * k_s, axis=-1, keepdims=True)
            return jnp.where(lane == s, col, acc)

        a_diag = lax.fori_loop(0, sb, body, jnp.zeros((sb, sb), F32))
        a_scr[r0:r0 + sb, r0:r0 + sb] = jnp.where(tril_sb, a_diag, 0.0)
        if blk > 0:
            ref = b[r0:r0 + 1]
            q_t = q_i * jnp.exp(b_i - ref)
            k_t = k[0:r0] * jnp.exp(ref - b[0:r0])
            a_scr[r0:r0 + sb, 0:r0] = _mm_nt(q_t, k_t)
        if blk < nsb - 1:
            a_scr[r0:r0 + sb, r0 + sb:c] = jnp.zeros((sb, c - r0 - sb), F32)
    o = o + _mm(a_scr[...], v)

    b_last = b[c - 1:c]
    k_dec = k * jnp.exp(b_last - b)
    eye = _iota2((dk, dk), 0) == _iota2((dk, dk), 1)
    d_col = jnp.sum(jnp.where(eye, jnp.broadcast_to(jnp.exp(b_last), (dk, dk)), 0.0),
                    axis=-1, keepdims=True)
    s_ref[...] = d_col * s0 + _mm_tn(k_dec, v)
    return o


def _even_mix_kernel(qa_ref, ka_ref, va_ref, ga_ref, lga_ref,
                     qb_ref, kb_ref, vb_ref, gb_ref, lgb_ref,
                     sg0_ref, sh0_ref, gn_ref, hn_ref,
                     o_ref, sg_ref, sh_ref,
                     sg_scr, sh_scr, b_scr, k_scr, a_scr, *, c, sb):
    step = pl.program_id(1)

    @pl.when(step == 0)
    def _():
        sg_scr[...] = sg0_ref[0]
        sh_scr[...] = sh0_ref[0]

    for h in range(H_A):
        o = _gated_chunk(qa_ref, ka_ref, va_ref, lga_ref, h * DK_A, DK_A, h * DV_A, DV_A,
                         sg_scr.at[h], b_scr, k_scr, a_scr, c, sb)
        vs = slice(h * DV_A, (h + 1) * DV_A)
        o_ref[:, vs] = _rms(o, gn_ref[...]) * ga_ref[:, vs]
    for h in range(H_B):
        o = _gated_chunk(qb_ref, kb_ref, vb_ref, lgb_ref, h * N_B, N_B, h * DV_B, DV_B,
                         sh_scr.at[h], b_scr, k_scr, a_scr, c, sb)
        vs = slice(h * DV_B, (h + 1) * DV_B)
        o_ref[:, MIX_A + h * DV_B:MIX_A + (h + 1) * DV_B] = _rms(o, hn_ref[...]) * gb_ref[:, vs]

    @pl.when(step == pl.num_programs(1) - 1)
    def _():
        sg_ref[0] = sg_scr[...]
        sh_ref[0] = sh_scr[...]


def _even_mix(pre, sg0, sh0, gn, hn, bsz, t_len, c, sb):
    nc = t_len // c
    m = bsz * t_len
    widths = (256, 256, 512, 512, 256, 512, 512, 512, 512, 512)
    row = lambda w: pl.BlockSpec((c, w), lambda b, s: (b * nc + s, 0))
    st = lambda shape: pl.BlockSpec((1,) + shape[1:], lambda b, s: (b, 0, 0, 0))
    return pl.pallas_call(
        functools.partial(_even_mix_kernel, c=c, sb=sb),
        out_shape=[jax.ShapeDtypeStruct((m, D_MODEL), F32),
                   jax.ShapeDtypeStruct(sg0.shape, F32),
                   jax.ShapeDtypeStruct(sh0.shape, F32)],
        grid=(bsz, nc),
        in_specs=[row(w) for w in widths] + [st(sg0.shape), st(sh0.shape),
                                             _const_spec(gn.shape), _const_spec(hn.shape)],
        out_specs=[row(D_MODEL), st(sg0.shape), st(sh0.shape)],
        scratch_shapes=[pltpu.VMEM(sg0.shape[1:], F32), pltpu.VMEM(sh0.shape[1:], F32),
                        pltpu.VMEM((c, LANE), F32), pltpu.VMEM((c, LANE), F32),
                        pltpu.VMEM((c, c), F32)],
        compiler_params=_cparams(("parallel", "arbitrary")),
        name="even_mix",
    )(*pre, sg0, sh0, gn, hn)


def _rwkv_pre_kernel(*refs, t_len, has_vmix):
    (h_ref, xl_ref, gm_ref, mu_ref, wr_ref, wk_ref, wv_ref, w0_ref, w1_ref, w2_ref,
     a0_ref, a1_ref, a2_ref, g1_ref, g2_ref, kk_ref, ka_ref) = refs[:17]
    pos = 17
    if has_vmix:
        vf_ref, v0_ref, v1_ref, v2_ref = refs[pos:pos + 4]
        pos += 4
    (r_out, lw_out, k_out, v_out, kkr_out, a_out, g_out, xn_out, carry) = refs[pos:]

    tm = h_ref.shape[0]
    xn = _rms(h_ref[...], gm_ref[...])
    xn_out[...] = xn
    rows = _iota2((tm, 1), 0)
    t_idx = (pl.program_id(0) * tm + rows) & (t_len - 1)
    prev = pltpu.roll(xn, 1, 0)
    prev = jnp.where(rows == 0, carry[0:1, :], prev)
    prev = jnp.where(t_idx == 0, xl_ref[...], prev)
    carry[0:1, :] = xn[tm - 1:tm, :]
    dx = prev - xn

    def mixed(i):
        return (xn + dx * mu_ref[i:i + 1, :]).astype(BF16)

    def dot(x, w_ref_):
        return jnp.dot(x, w_ref_[...], preferred_element_type=F32)

    r = dot(mixed(0), wr_ref)
    r_out[...] = r
    dec = w0_ref[...] + _mm(jnp.tanh(dot(mixed(1), w1_ref)), w2_ref[...])
    lw_out[...] = -_sigmoid(dec) * math.exp(-0.5)
    k = dot(mixed(2), wk_ref)
    xv = mixed(3)
    v = dot(xv, wv_ref)
    a = _sigmoid(a0_ref[...] + _mm(dot(mixed(4), a1_ref), a2_ref[...]))
    a_out[...] = a
    g_out[...] = _mm(_sigmoid(dot(mixed(5), g1_ref)), g2_ref[...])
    kkr_out[...] = k * kk_ref[...]
    k_out[...] = k * (1.0 + (a - 1.0) * ka_ref[...])
    if has_vmix:
        gate = _sigmoid(v0_ref[...] + _mm(dot(xv, v1_ref), v2_ref[...]))
        v = v + (vf_ref[...] - v) * gate
    v_out[...] = v


def _rwkv_pre(h, xl, t_len, wts, vmix, tm):
    m = h.shape[0]
    assert t_len & (t_len - 1) == 0
    row = pl.BlockSpec((tm, D_MODEL), lambda i: (i, 0))
    args = [h, xl] + list(wts)
    specs = [row, row] + [_const_spec(w.shape) for w in wts]
    if vmix is not None:
        args += list(vmix)
        specs += [row] + [_const_spec(w.shape) for w in vmix[1:]]
    return pl.pallas_call(
        functools.partial(_rwkv_pre_kernel, t_len=t_len, has_vmix=vmix is not None),
        out_shape=[jax.ShapeDtypeStruct((m, D_MODEL), F32)] * 8,
        grid=(m // tm,),
        in_specs=specs,
        out_specs=[row] * 8,
        scratch_shapes=[pltpu.VMEM((8, D_MODEL), F32)],
        compiler_params=_cparams(("arbitrary",)),
        name="rwkv_pre",
    )(*args)


def _unit_lower_inverse(l_mat, c, sb):
    row = _iota2((c, c), 0)
    col = _iota2((c, c), 1)
    eye = (row == col).astype(F32)
    if c == sb:
        l_bd = l_mat
    else:
        same = (row // sb) == (col // sb)
        l_bd = jnp.where(same, l_mat, 0.0)
    p = eye - l_bd
    l_pow = _mmh(l_bd, l_bd)
    n_fac = int(math.log2(sb)) - 1
    for i in range(n_fac):
        p = p + _mmh(p, l_pow)
        if i < n_fac - 1:
            l_pow = _mmh(l_pow, l_pow)
    if c == sb:
        return p
    assert c // sb == 4
    n = _mmh(p, l_mat - l_bd)
    n2 = _mmh(n, n)
    q = eye - n
    q = q + _mmh(q, n2)
    return _mmh(q, p)


def _rwkv_mix_kernel(r_ref, lw_ref, k_ref, v_ref, kkr_ref, a_ref, g_ref, s0_ref,
                     rk_ref, lnw_ref, lnb_ref, o_ref, s_out_ref, s_scr, *, c, sb):
    step = pl.program_id(1)

    @pl.when(step == 0)
    def _():
        s_scr[...] = s0_ref[0]

    row = _iota2((c, c), 0)
    col = _iota2((c, c), 1)
    strict = col < row
    incl = col <= row
    tri = incl.astype(BF16)

    for h in range(H_C):
        sl = slice(h * N_C, (h + 1) * N_C)
        r = r_ref[:, sl]
        lw = lw_ref[:, sl]
        k = k_ref[:, sl]
        v = v_ref[:, sl]
        a = a_ref[:, sl]
        kk = kkr_ref[:, sl]
        kk = kk * lax.rsqrt(jnp.maximum(jnp.sum(kk * kk, axis=-1, keepdims=True), KK_EPS))

        cw = _cumsum_rows(lw, tri)
        w_tot = cw[c - 1:c]
        a_t = kk * jnp.exp(cw - lw)
        r_t = r * jnp.exp(cw)
        e_inv = jnp.exp(-cw)
        b_t = kk * a * e_inv
        k_t = k * e_inv
        e_end = jnp.exp(w_tot - cw)
        b_e = kk * a * e_end
        k_e = k * e_end

        l_ab = jnp.where(strict, _mmh_nt(a_t, b_t), 0.0)
        l_ak = jnp.where(strict, _mmh_nt(a_t, k_t), 0.0)
        m_rb = jnp.where(incl, _mmh_nt(r_t, b_t), 0.0)
        m_rk = jnp.where(incl, _mmh_nt(r_t, k_t), 0.0)
        t_inv = _unit_lower_inverse(l_ab, c, sb)

        s0 = s_scr[h]
        u = -_mmh(t_inv, _mm_nt(a_t, s0) + _mm(l_ak, v))
        o = _mm_nt(r_t, s0) + _mm(m_rb, u) + _mm(m_rk, v)
        s_scr[h] = s0 * jnp.exp(w_tot) + _mm_tn(u, b_e) + _mm_tn(v, k_e)

        mean = jnp.mean(o, axis=-1, keepdims=True)
        var = jnp.mean(jnp.square(o - mean), axis=-1, keepdims=True)
        on = (o - mean) * lax.rsqrt(var + GN_EPS) * lnw_ref[:, sl] + lnb_ref[:, sl]
        bonus = jnp.sum(r * k * rk_ref[:, sl], axis=-1, keepdims=True) * v
        o_ref[:, sl] = (on + bonus) * g_ref[:, sl]

    @pl.when(step == pl.num_programs(1) - 1)
    def _():
        s_out_ref[0] = s_scr[...]


def _rwkv_mix(pre, s0, rk, lnw, lnb, bsz, t_len, c, sb):
    nc = t_len // c
    m = bsz * t_len
    row = pl.BlockSpec((c, D_MODEL), lambda b, s: (b * nc + s, 0))
    st = pl.BlockSpec((1,) + s0.shape[1:], lambda b, s: (b, 0, 0, 0))
    return pl.pallas_call(
        functools.partial(_rwkv_mix_kernel, c=c, sb=sb),
        out_shape=[jax.ShapeDtypeStruct((m, D_MODEL), F32), jax.ShapeDtypeStruct(s0.shape, F32)],
        grid=(bsz, nc),
        in_specs=[row] * 7 + [st, _const_spec(rk.shape), _const_spec(lnw.shape),
                              _const_spec(lnb.shape)],
        out_specs=[row, st],
        scratch_shapes=[pltpu.VMEM(s0.shape[1:], F32)],
        compiler_params=_cparams(("parallel", "arbitrary")),
        name="rwkv_mix",
    )(*pre, s0, rk, lnw, lnb)


def _post_kernel(h_ref, o_ref, wo_ref, gf_ref, wg_ref, wu_ref, wd_ref, gp_ref, wpg_ref,
                 p_ref, wple_ref, gfin_ref, out_ref, *, final):
    h1 = h_ref[...] + jnp.dot(o_ref[...].astype(BF16), wo_ref[...], preferred_element_type=F32)
    x2 = _rms(h1, gf_ref[...]).astype(BF16)
    h2 = h1
    for lo in range(0, FFN_DIM, FFN_SPLIT):
        gate = jnp.dot(x2, wg_ref[:, lo:lo + FFN_SPLIT], preferred_element_type=F32)
        up = jnp.dot(x2, wu_ref[:, lo:lo + FFN_SPLIT], preferred_element_type=F32)
        act = (_silu(gate) * up).astype(BF16)
        h2 = h2 + jnp.dot(act, wd_ref[lo:lo + FFN_SPLIT, :], preferred_element_type=F32)
    x3 = _rms(h2, gp_ref[...]).astype(BF16)
    gate = _sigmoid(jnp.dot(x3, wpg_ref[...], preferred_element_type=F32))
    emb = jnp.dot(p_ref[...].astype(BF16), wple_ref[...], preferred_element_type=F32)
    h3 = h2 + gate * emb
    out_ref[...] = _rms(h3, gfin_ref[...]) if final else h3


def _post(h, o, p, wts, final, tm):
    m = h.shape[0]
    wo, gf, wg, wu, wd, gp, wpg, wple, gfin = wts
    row = lambda w: pl.BlockSpec((tm, w), lambda i: (i, 0))
    return pl.pallas_call(
        functools.partial(_post_kernel, final=final),
        out_shape=jax.ShapeDtypeStruct((m, D_MODEL), F32),
        grid=(m // tm,),
        in_specs=[row(D_MODEL), row(D_MODEL), _const_spec(wo.shape), _const_spec(gf.shape),
                  _const_spec(wg.shape), _const_spec(wu.shape), _const_spec(wd.shape),
                  _const_spec(gp.shape), _const_spec(wpg.shape), row(PLE_DIM),
                  _const_spec(wple.shape), _const_spec(gfin.shape)],
        out_specs=row(D_MODEL),
        compiler_params=_cparams(("parallel",)),
        name="post",
    )(h, o, wo, gf, wg, wu, wd, gp, wpg, p, wple, gfin)


def _pad_cols(w, n):
    return jnp.pad(w, ((0, 0), (0, n - w.shape[1])))


def _pad_rows(w, n):
    return jnp.pad(w, ((0, n - w.shape[0]), (0, 0)))


def _pad_time(a, bsz, t_len, t_pad):
    if t_pad == t_len:
        return a
    a = a.reshape(bsz, t_len, a.shape[-1])
    a = jnp.pad(a, ((0, 0), (0, t_pad - t_len), (0, 0)))
    return a.reshape(bsz * t_pad, a.shape[-1])


def _unpad_time(a, bsz, t_len, t_pad):
    if t_pad == t_len:
        return a
    return a.reshape(bsz, t_pad, a.shape[-1])[:, :t_len].reshape(bsz * t_len, a.shape[-1])


def _run_group(x, p, s_gla, s_hgrn, s_rwkv, s_shift, wt):
    bsz, t_len, _ = x.shape
    m = bsz * t_len
    if t_len % CHUNK == 0:
        c, t_pad = CHUNK, t_len
    else:
        c = SUB
        t_pad = -(-t_len // SUB) * SUB
    tm = min(512, m)
    tm_rw = min(256, m)
    h = x.reshape(m, D_MODEL)
    out_gla, out_hgrn, out_rwkv, out_shift = [], [], [], []
    v_first = None
    row1 = lambda v: v.reshape(1, -1)
    for i in range(DEPTH):
        j = i // 2
        gm = row1(wt['norm_mix'][i])
        if i % 2 == 0:
            pre = _even_pre(h, gm, wt['w_cat'][j], wt['w_gk1'][j], wt['w_gk2'][j],
                            row1(wt['b_gk'][j]), wt['hgrn_gamma'], j, tm)
            pre = [_pad_time(a, bsz, t_len, t_pad) for a in pre]
            o, sg, sh = _even_mix(pre, s_gla[j], s_hgrn[j], row1(wt['gla_norm'][j]),
                                  row1(wt['hgrn_norm'][j]), bsz, t_pad, c, SUB)
            out_gla.append(sg)
            out_hgrn.append(sh)
            w_o = wt['w_out_even'][j]
        else:
            xl = jnp.broadcast_to(s_shift[j][:, None, :], (bsz, t_len, D_MODEL)).reshape(m, D_MODEL)
            wts = [gm, wt['rw_mu'][j], wt['rw_wr'][j], wt['rw_wk'][j], wt['rw_wv'][j],
                   row1(wt['rw_w0'][j]), wt['rw_w1'][j], wt['rw_w2'][j],
                   row1(wt['rw_a0'][j]), wt['rw_a1'][j], wt['rw_a2'][j],
                   wt['rw_g1'][j], wt['rw_g2'][j], row1(wt['rw_kk'][j]), row1(wt['rw_ka'][j])]
            vmix = None
            if j > 0:
                vmix = [v_first, row1(wt['rw_v0'][j - 1]), wt['rw_v1'][j - 1], wt['rw_v2'][j - 1]]
            r, lw, k, v, kkr, a, g, xn = _rwkv_pre(h, xl, t_len, wts, vmix, tm_rw)
            if j == 0:
                v_first = v
            pre = [_pad_time(z, bsz, t_len, t_pad) for z in (r, lw, k, v, kkr, a, g)]
            o, sr = _rwkv_mix(pre, s_rwkv[j], row1(wt['rw_rk'][j]), row1(wt['rw_lnw'][j]),
                              row1(wt['rw_lnb'][j]), bsz, t_pad, c, SUB)
            out_rwkv.append(sr)
            out_shift.append(xn.reshape(bsz, t_len, D_MODEL)[:, -1])
            w_o = wt['rw_wo'][j]
        o = _unpad_time(o, bsz, t_len, t_pad)
        post_w = [w_o, row1(wt['norm_ffn'][i]), wt['w_ffn_gate'][i], wt['w_ffn_up'][i],
                  wt['w_ffn_down'][i], row1(wt['norm_ple'][i]), wt['w_ple_gate'][i],
                  wt['w_ple'][i], row1(wt['norm_final'])]
        h = _post(h, o, p[i].reshape(m, PLE_DIM), post_w, i == DEPTH - 1, tm)
    return (h.reshape(bsz, t_len, D_MODEL), jnp.stack(out_gla), jnp.stack(out_hgrn),
            jnp.stack(out_rwkv), jnp.stack(out_shift))


def kernel(x_prompt, x_sample, p_prompt, p_sample, state_gla, state_hgrn, state_rwkv, state_shift, norm_mix, norm_ffn, norm_ple, norm_final, w_in_even, w_gk2, b_gk, gla_norm, hgrn_gamma, hgrn_norm, w_out_even, rw_mu, rw_wr, rw_wk, rw_wv, rw_wo, rw_w0, rw_w1, rw_w2, rw_a0, rw_a1, rw_a2, rw_v0, rw_v1, rw_v2, rw_g1, rw_g2, rw_kk, rw_ka, rw_rk, rw_lnw, rw_lnb, w_ffn_gate, w_ffn_up, w_ffn_down, w_ple, w_ple_gate):
    bf = lambda w: w.astype(BF16)
    w_cat = jnp.concatenate([w_in_even[:, :, 0:1536], w_in_even[:, :, 1552:3600]], axis=-1)
    w_gk1 = jnp.pad(w_in_even[:, :, 1536:1552], ((0, 0), (0, 0), (0, LANE - GK_RANK)))
    w_gk2p = jnp.pad(w_gk2, ((0, 0), (0, LANE - GK_RANK), (0, 0)))
    pad_c = lambda w: jnp.pad(w, ((0, 0), (0, 0), (0, LANE - w.shape[2])))
    pad_r = lambda w: jnp.pad(w, ((0, 0), (0, LANE - w.shape[1]), (0, 0)))
    wt = dict(
        norm_mix=norm_mix, norm_ffn=norm_ffn, norm_ple=norm_ple, norm_final=norm_final,
        w_cat=bf(w_cat), w_gk1=bf(w_gk1), w_gk2=bf(w_gk2p), b_gk=b_gk, gla_norm=gla_norm,
        hgrn_gamma=hgrn_gamma, hgrn_norm=hgrn_norm, w_out_even=bf(w_out_even),
        rw_mu=rw_mu, rw_wr=bf(rw_wr), rw_wk=bf(rw_wk), rw_wv=bf(rw_wv), rw_wo=bf(rw_wo),
        rw_w0=rw_w0, rw_w1=bf(pad_c(rw_w1)), rw_w2=bf(pad_r(rw_w2)),
        rw_a0=rw_a0, rw_a1=bf(pad_c(rw_a1)), rw_a2=bf(pad_r(rw_a2)),
        rw_v0=rw_v0, rw_v1=bf(pad_c(rw_v1)), rw_v2=bf(pad_r(rw_v2)),
        rw_g1=bf(rw_g1), rw_g2=bf(rw_g2), rw_kk=rw_kk, rw_ka=rw_ka, rw_rk=rw_rk,
        rw_lnw=rw_lnw, rw_lnb=rw_lnb,
        w_ffn_gate=bf(w_ffn_gate), w_ffn_up=bf(w_ffn_up), w_ffn_down=bf(w_ffn_down),
        w_ple=bf(w_ple), w_ple_gate=bf(w_ple_gate),
    )
    bp =x_prompt.shape[0]
    zeros = lambda s: jnp.zeros((s.shape[0], bp) + s.shape[2:], F32)
    y_p, gla_p, hgrn_p, rwkv_p, shift_p = _run_group(
        x_prompt, p_prompt, zeros(state_gla), zeros(state_hgrn), zeros(state_rwkv),
        zeros(state_shift), wt)
    y_s, gla_s, hgrn_s, rwkv_s, shift_s = _run_group(
        x_sample, p_sample, state_gla, state_hgrn, state_rwkv, state_shift, wt)
    return (y_p, y_s, gla_p, hgrn_p, rwkv_p, shift_p, gla_s, hgrn_s, rwkv_s, shift_s)
```

```python
import functools
import math

import numpy as np
import jax
import jax.numpy as jnp
from jax import lax
from jax.experimental import pallas as pl
from jax.experimental.pallas import tpu as pltpu

F32 = jnp.float32
BF16 = jnp.bfloat16

D_MODEL = 1024
DEPTH = 4
MIX_A = 512
H_A = 4
DV_A = 128
DK_A = 64
GK_RANK = 16
GK_NORM = 16.0
H_B = 4
DV_B = 128
N_B = 128
LB_FLOOR = 1e-20
N_C = 64
H_C = D_MODEL // N_C
FFN_DIM = 2816
PLE_DIM = 256
RMS_EPS = 1e-6
GN_EPS = 64e-5
KK_EPS = 1e-24

LANE = 128
CHUNK = 64
SUB = 16
VMEM_LIMIT = 56 * 1024 * 1024
FFN_SPLIT = 1408
PAIR_V = 2 * DV_A
SEG = 256

EVEN_WIDTHS = (256, 256, 512, 512, 256, 512, 512, 512, 512, 512)


def _cparams(sem):
    return pltpu.CompilerParams(dimension_semantics=sem, vmem_limit_bytes=VMEM_LIMIT)


def _const_spec(shape):
    nd = len(shape)
    return pl.BlockSpec(shape, lambda *_: (0,) * nd, pipeline_mode=pl.Buffered(1))


def _rms(x, g, eps=RMS_EPS):
    return x * lax.rsqrt(jnp.mean(x * x, axis=-1, keepdims=True) + eps) * g


def _sigmoid(x):
    return 1.0 / (1.0 + jnp.exp(-x))


def _silu(x):
    return x * _sigmoid(x)


def _log_sigmoid(x):
    return jnp.minimum(x, 0.0) - jnp.log1p(jnp.exp(-jnp.abs(x)))


def _dot(a, b):
    return jnp.dot(a, b, preferred_element_type=F32)


def _dot_nt(a, b):
    return lax.dot_general(a, b, (((1,), (1,)), ((), ())), preferred_element_type=F32)


def _dot_tn(a, b):
    return lax.dot_general(a, b, (((0,), (0,)), ((), ())), preferred_element_type=F32)


def _mm(a, b):
    return _dot(a.astype(BF16), b.astype(BF16))


def _split2(x):
    hi = x.astype(BF16)
    return hi, (x - hi.astype(F32)).astype(BF16)


def _split3(x):
    hi = x.astype(BF16)
    r1 = x - hi.astype(F32)
    mid = r1.astype(BF16)
    return hi, mid, (r1 - mid.astype(F32)).astype(BF16)


def _sum01_left(m01, x):
    hi, mid, lo = _split3(x)
    return _dot(m01, hi) + _dot(m01, mid) + _dot(m01, lo)


def _sum01_right(x, m01):
    hi, mid, lo = _split3(x)
    return _dot(hi, m01) + _dot(mid, m01) + _dot(lo, m01)


def _iota2(shape, axis):
    return lax.broadcasted_iota(jnp.int32, shape, axis)


def _stack_heads(x, first):
    return jnp.concatenate([jnp.where(first, x, 0), jnp.where(first, 0, x)], axis=0)


def _lockstep(gens):
    gens = list(gens)
    while gens:
        alive = []
        for g in gens:
            try:
                next(g)
                alive.append(g)
            except StopIteration:
                pass
        gens = alive


def _load_rows(ref, scr, c):
    t = ref.shape[1]
    if t == c:
        return ref[0]
    scr[...] = jnp.zeros(scr.shape, F32)
    scr[0:t, :] = ref[0]
    return scr[...]


def _even_pre_kernel(h_ref, gm_ref, w_ref, wgk1_ref, wgk2_ref, bgk_ref, gam_ref,
                     qa_ref, ka_ref, va_ref, ga_ref, lga_ref,
                     qb_ref, kb_ref, vb_ref, gb_ref, lgb_ref, *, layer):
    xn = _rms(h_ref[...], gm_ref[...]).astype(BF16)

    def proj(lo, hi):
        return _dot(xn, w_ref[:, lo:hi])

    qa_ref[...] = proj(0, 256) * DK_A ** -0.5
    ka_ref[...] = proj(256, 512)
    va_ref[...] = proj(512, 1024)
    ga_ref[...] = _silu(proj(1024, 1536))
    gk_lr = _dot(xn, wgk1_ref[...])
    gk = _mm(gk_lr, wgk2_ref[...]) + bgk_ref[...]
    lga_ref[...] = _log_sigmoid(gk) / GK_NORM

    gam = gam_ref[...]
    n_even = gam.shape[0]
    gmax = gam[0:1]
    for i in range(1, n_even):
        gmax = jnp.maximum(gmax, gam[i:i + 1])
    es = [jnp.exp(gam[i:i + 1] - gmax) for i in range(n_even)]
    den = es[0]
    for i in range(1, n_even):
        den = den + es[i]
    sm = [e / den for e in es]
    cum = sm[0]
    for i in range(1, layer + 1):
        cum = cum + sm[i]
    lb = cum - sm[0]

    qb_ref[...] = _silu(proj(1536, 2048)) * N_B ** -0.5
    z = proj(2048, 2560)
    lgb_ref[...] = jnp.log(jnp.maximum(lb, LB_FLOOR) + (1.0 - lb) * _sigmoid(z))
    kb_ref[...] = (1.0 - lb) * _sigmoid(-z)
    vb_ref[...] = proj(2560, 3072)
    gb_ref[...] = _silu(proj(3072, 3584))


def _even_pre(h, gm, w_cat, wgk1, wgk2, bgk, gamma, layer, tm):
    m = h.shape[0]
    row = lambda w: pl.BlockSpec((tm, w), lambda i: (i, 0))
    return pl.pallas_call(
        functools.partial(_even_pre_kernel, layer=layer),
        out_shape=[jax.ShapeDtypeStruct((m, w), F32) for w in EVEN_WIDTHS],
        grid=(m // tm,),
        in_specs=[row(D_MODEL), _const_spec(gm.shape), _const_spec(w_cat.shape),
                  _const_spec(wgk1.shape), _const_spec(wgk2.shape), _const_spec(bgk.shape),
                  _const_spec(gamma.shape)],
        out_specs=[row(w) for w in EVEN_WIDTHS],
        compiler_params=_cparams(("parallel",)),
        name="even_pre",
    )(h, gm, w_cat, wgk1, wgk2, bgk, gamma)


def _level_widths(c):
    return [c >> i for i in range(1, int(math.log2(c)) + 1)]


def _decay_sum_matrix(c):
    i = np.arange(c)[:, None]
    t = np.arange(c)[None, :]
    blocks = [t <= i, t > i]
    for w in _level_widths(c):
        ref = (i // (2 * w)) * (2 * w) + w
        blocks.append(np.where(i >= ref, (t > ref) & (t <= i), (t > i) & (t <= ref)))
    return np.concatenate(blocks, axis=0).astype(np.float32)


def _level_index(c):
    t = np.arange(c)[:, None]
    s = np.arange(c)[None, :]
    lev = np.full((c, c), -1, np.int32)
    lev[t == s] = 0
    for li, w in enumerate(_level_widths(c)):
        m = (t // (2 * w) == s // (2 * w)) & (t % (2 * w) >= w) & (s % (2 * w) < w)
        lev[m] = li + 1
    return np.tile(lev, (1, 2))


def _gated_pair(q, k, v, lg, e_all, lev, ones_cv, y_ref, out, c):
    kw = q.shape[1]
    nlev = len(_level_widths(c))
    first_k = _iota2((1, kw), 1) < kw // 2
    first_v = _iota2((1, PAIR_V), 1) < PAIR_V // 2
    parts = []
    for l in range(nlev + 1):
        if l == 0:
            q_l, k_l = q, k
        else:
            e = e_all[(l + 1) * c:(l + 2) * c]
            q_l, k_l = q * e, k * e
        parts.append(_dot_nt(q_l.astype(BF16), _stack_heads(k_l.astype(BF16), first_k)))
    y0 = y_ref[...]
    o_inter = _mm(q * e_all[0:c], y0)
    upd = _dot_tn((k * e_all[c:2 * c]).astype(BF16), v.astype(BF16))
    l_hi, l_mid, l_lo = _split3(lg)
    b_last = _dot_tn(l_hi, ones_cv) + _dot_tn(l_mid, ones_cv) + _dot_tn(l_lo, ones_cv)
    yield
    a_mat = jnp.where(lev == 0, parts[0], 0.0)
    for l in range(1, nlev + 1):
        a_mat = jnp.where(lev == l, parts[l], a_mat)
    o = o_inter + _dot(a_mat.astype(BF16), _stack_heads(v.astype(BF16), first_v))
    same_head = (_iota2((kw, 1), 0) // (kw // 2)) == (_iota2((1, PAIR_V), 1) // (PAIR_V // 2))
    y_ref[...] = jnp.exp(b_last) * y0 + jnp.where(same_head, upd, 0.0)
    out.append(o)


def _even_mix_kernel(*refs, c):
    (qa_ref, ka_ref, va_ref, ga_ref, lga_ref, qb_ref, kb_ref, vb_ref, gb_ref, lgb_ref,
     msum_ref, lev_ref, ones_ref, sg0_ref, sh0_ref, gn_ref, hn_ref,
     o_ref, sg_ref, sh_ref, ya_scr, yb_scr) = refs[:22]
    pads = refs[22:]
    step = pl.program_id(1)
    ka_w = 2 * DK_A
    kb_w = 2 * N_B

    @pl.when(step == 0)
    def _():
        ya_scr[...] = jnp.zeros(ya_scr.shape, F32)
        yb_scr[...] = jnp.zeros(yb_scr.shape, F32)
        for p in range(H_A // 2):
            ya_scr[p, 0:DK_A, 0:DV_A] = sg0_ref[0, 2 * p]
            ya_scr[p, DK_A:ka_w, DV_A:PAIR_V] = sg0_ref[0, 2 * p + 1]
        for p in range(H_B // 2):
            yb_scr[p, 0:N_B, 0:DV_B] = sh0_ref[0, 2 * p]
            yb_scr[p, N_B:kb_w, DV_B:PAIR_V] = sh0_ref[0, 2 * p + 1]

    ins = (qa_ref, ka_ref, va_ref, ga_ref, lga_ref, qb_ref, kb_ref, vb_ref, gb_ref, lgb_ref)
    if pads:
        qa, ka, va, ga, lga, qb, kb, vb, gb, lgb = [_load_rows(r, s, c) for r, s in zip(ins, pads)]
    else:
        qa, ka, va, ga, lga, qb, kb, vb, gb, lgb = [r[0] for r in ins]
    t_out = o_ref.shape[1]

    msum = msum_ref[...]
    lev = lev_ref[...]
    ones_cv = ones_ref[...]
    ea = jnp.exp(_sum01_left(msum, lga))
    eb = jnp.exp(_sum01_left(msum, lgb))

    outs_a = [[] for _ in range(H_A // 2)]
    outs_b = [[] for _ in range(H_B // 2)]
    gens = []
    for p in range(H_A // 2):
        ks = slice(p * ka_w, (p + 1) * ka_w)
        vs = slice(p * PAIR_V, (p + 1) * PAIR_V)
        gens.append(_gated_pair(qa[:, ks], ka[:, ks], va[:, vs], lga[:, ks], ea[:, ks], lev,
                                ones_cv, ya_scr.at[p], outs_a[p], c))
    for p in range(H_B // 2):
        ks = slice(p * kb_w, (p + 1) * kb_w)
        vs = slice(p * PAIR_V, (p + 1) * PAIR_V)
        gens.append(_gated_pair(qb[:, ks], kb[:, ks], vb[:, vs], lgb[:, ks], eb[:, ks], lev,
                                ones_cv, yb_scr.at[p], outs_b[p], c))
    _lockstep(gens)

    for h in range(H_A):
        o = outs_a[h // 2][0][:, (h % 2) * DV_A:(h % 2 + 1) * DV_A]
        res = _rms(o, gn_ref[...]) * ga[:, h * DV_A:(h + 1) * DV_A]
        o_ref[0, :, h * DV_A:(h + 1) * DV_A] = res[0:t_out]
    for h in range(H_B):
        o = outs_b[h // 2][0][:, (h % 2) * DV_B:(h % 2 + 1) * DV_B]
        res = _rms(o, hn_ref[...]) * gb[:, h * DV_B:(h + 1) * DV_B]
        o_ref[0, :, MIX_A + h * DV_B:MIX_A + (h + 1) * DV_B] = res[0:t_out]

    @pl.when(step == pl.num_programs(1) - 1)
    def _():
        for p in range(H_A // 2):
            sg_ref[0, 2 * p] = ya_scr[p, 0:DK_A, 0:DV_A]
            sg_ref[0, 2 * p + 1] = ya_scr[p, DK_A:ka_w, DV_A:PAIR_V]
        for p in range(H_B // 2):
            sh_ref[0, 2 * p] = yb_scr[p, 0:N_B, 0:DV_B]
            sh_ref[0, 2 * p + 1] = yb_scr[p, N_B:kb_w, DV_B:PAIR_V]


def _even_mix(pre, sg0, sh0, gn, hn, bsz, t_len):
    c = CHUNK
    tb = min(t_len, c)
    nc = -(-t_len // c)
    assert t_len == nc * tb
    pre = [a.reshape(bsz, t_len, a.shape[-1]) for a in pre]
    msum = jnp.asarray(_decay_sum_matrix(c), BF16)
    lev = jnp.asarray(_level_index(c))
    ones_cv = jnp.ones((c, PAIR_V), BF16)
    row = lambda w: pl.BlockSpec((1, tb, w), lambda b, s: (b, s, 0))
    st = lambda shape: pl.BlockSpec((1,) + shape[1:], lambda b, s: (b, 0, 0, 0))
    scratch = [pltpu.VMEM((H_A // 2, 2 * DK_A, PAIR_V), F32),
               pltpu.VMEM((H_B // 2, 2 * N_B, PAIR_V), F32)]
    if tb < c:
        scratch += [pltpu.VMEM((c, w), F32) for w in EVEN_WIDTHS]
    o, sg, sh = pl.pallas_call(
        functools.partial(_even_mix_kernel, c=c),
        out_shape=[jax.ShapeDtypeStruct((bsz, t_len, D_MODEL), F32),
                   jax.ShapeDtypeStruct(sg0.shape, F32),
                   jax.ShapeDtypeStruct(sh0.shape, F32)],
        grid=(bsz, nc),
        in_specs=[row(w) for w in EVEN_WIDTHS] + [
            _const_spec(msum.shape), _const_spec(lev.shape), _const_spec(ones_cv.shape),
            st(sg0.shape), st(sh0.shape), _const_spec(gn.shape), _const_spec(hn.shape)],
        out_specs=[row(D_MODEL), st(sg0.shape), st(sh0.shape)],
        scratch_shapes=scratch,
        compiler_params=_cparams(("parallel", "arbitrary")),
        name="even_mix",
    )(*pre, msum, lev, ones_cv, sg0, sh0, gn, hn)
    return o.reshape(bsz * t_len, D_MODEL), sg, sh


def _rwkv_pre_kernel(*refs, t_len, has_vmix):
    (h_ref, xl_ref, gm_ref, mu_ref, wr_ref, wk_ref, wv_ref, w0_ref, w1_ref, w2_ref,
     a0_ref, a1_ref, a2_ref, g1_ref, g2_ref, kk_ref, ka_ref) = refs[:17]
    pos = 17
    if has_vmix:
        vf_ref, v0_ref, v1_ref, v2_ref = refs[pos:pos + 4]
        pos += 4
    (r_out, lw_out, k_out, v_out, kkr_out, a_out, g_out, xn_out, carry) = refs[pos:]

    tm = h_ref.shape[0]
    xn = _rms(h_ref[...], gm_ref[...])
    xn_out[...] = xn
    rows = _iota2((tm, 1), 0)
    t_idx = (pl.program_id(0) * tm + rows) & (t_len - 1)
    prev = pltpu.roll(xn, 1, 0)
    prev = jnp.where(rows == 0, carry[0:1, :], prev)
    prev = jnp.where(t_idx == 0, xl_ref[...], prev)
    carry[0:1, :] = xn[tm - 1:tm, :]
    dx = prev - xn

    def mixed(i):
        return (xn + dx * mu_ref[i:i + 1, :]).astype(BF16)

    r = _dot(mixed(0), wr_ref[...])
    r_out[...] = r
    dec = w0_ref[...] + _mm(jnp.tanh(_dot(mixed(1), w1_ref[...])), w2_ref[...])
    lw_out[...] = -_sigmoid(dec) * math.exp(-0.5)
    k = _dot(mixed(2), wk_ref[...])
    xv = mixed(3)
    v = _dot(xv, wv_ref[...])
    a = _sigmoid(a0_ref[...] + _mm(_dot(mixed(4), a1_ref[...]), a2_ref[...]))
    a_out[...] = a
    g_out[...] = _mm(_sigmoid(_dot(mixed(5), g1_ref[...])), g2_ref[...])
    kkr_out[...] = k * kk_ref[...]
    k_out[...] = k * (1.0 + (a - 1.0) * ka_ref[...])
    if has_vmix:
        gate = _sigmoid(v0_ref[...] + _mm(_dot(xv, v1_ref[...]), v2_ref[...]))
        v = v + (vf_ref[...] - v) * gate
    v_out[...] = v


def _rwkv_pre(h, xl, t_len, wts, vmix, tm):
    m = h.shape[0]
    assert t_len & (t_len - 1) == 0
    row = pl.BlockSpec((tm, D_MODEL), lambda i: (i, 0))
    args = [h, xl] + list(wts)
    specs = [row, row] + [_const_spec(w.shape) for w in wts]
    if vmix is not None:
        args += list(vmix)
        specs += [row] + [_const_spec(w.shape) for w in vmix[1:]]
    return pl.pallas_call(
        functools.partial(_rwkv_pre_kernel, t_len=t_len, has_vmix=vmix is not None),
        out_shape=[jax.ShapeDtypeStruct((m, D_MODEL), F32)] * 8,
        grid=(m // tm,),
        in_specs=specs,
        out_specs=[row] * 8,
        scratch_shapes=[pltpu.VMEM((8, D_MODEL), F32)],
        compiler_params=_cparams(("arbitrary",)),
        name="rwkv_pre",
    )(*args)


def _mm3_pair(x, y, first):
    xh, xl = _split2(x)
    yh, yl = _split2(y)
    ybh = _stack_heads(yh, first)
    ybl = _stack_heads(yl, first)
    return _dot(xh, ybh) + _dot(xh, ybl) + _dot(xl, ybh)


def _segsum(x, seg):
    c = x.shape[0]
    n = D_MODEL // SEG
    xs = jnp.concatenate([x[:, i * SEG:(i + 1) * SEG] for i in range(n)], axis=0)
    ys = _sum01_right(xs, seg)
    return jnp.concatenate([ys[i * c:(i + 1) * c] for i in range(n)], axis=1)


def _rwkv_pair(a_t, r_t, b_t, k_t, b_e, k_e, v, w_tot, y_ref, out, c):
    t_i = _iota2((c, LANE), 0)
    s_i = _iota2((c, LANE), 1) & (c - 1)
    first = _iota2((1, LANE), 1) < N_C
    strict = s_i < t_i
    incl = s_i <= t_i
    eye = (s_i == t_i).astype(F32)
    same_blk = (s_i // SUB) == (t_i // SUB)

    lhs = jnp.concatenate([a_t, r_t], axis=0).astype(BF16)
    rhs = jnp.concatenate([_stack_heads(b_t.astype(BF16), first),
                           _stack_heads(k_t.astype(BF16), first)], axis=0)
    m1 = _dot_nt(lhs, rhs)
    l_ab = jnp.where(strict, m1[0:c, 0:LANE], 0.0)
    l_ak = jnp.where(strict, m1[0:c, LANE:2 * LANE], 0.0)
    m_rb = jnp.where(incl, m1[c:2 * c, 0:LANE], 0.0)
    m_rk = jnp.where(incl, m1[c:2 * c, LANE:2 * LANE], 0.0)
    y0 = y_ref[...]
    xs = _dot_nt(lhs, y0.astype(BF16))
    v_bd = _stack_heads(v.astype(BF16), first)
    rhs0 = xs[0:c] + _dot(l_ak.astype(BF16), v_bd)
    yield
    l_bd = jnp.where(same_blk, l_ab, 0.0)
    l_off = l_ab - l_bd
    l2 = _mm3_pair(l_bd, l_bd, first)
    p = eye - l_bd
    yield
    both = _mm3_pair(jnp.concatenate([p, l2], axis=0), l2, first)
    p = p + both[0:c]
    l4 = both[c:2 * c]
    yield
    both = _mm3_pair(jnp.concatenate([p, l4], axis=0), l4, first)
    p = p + both[0:c]
    l8 = both[c:2 * c]
    yield
    p = p + _mm3_pair(p, l8, first)
    yield
    n = _mm3_pair(p, l_off, first)
    yield
    n2 = _mm3_pair(n, n, first)
    q = eye - n
    yield
    q = q + _mm3_pair(q, n2, first)
    yield
    w = _mm3_pair(p, rhs0, first)
    yield
    u = -_mm3_pair(q, w, first)
    yield
    mr = jnp.concatenate([m_rb, m_rk], axis=1).astype(BF16)
    uv_bd = jnp.concatenate([_stack_heads(u.astype(BF16), first), v_bd], axis=0)
    out.append(xs[c:2 * c] + _dot(mr, uv_bd))
    uv = jnp.concatenate([u, v], axis=0).astype(BF16)
    bk = jnp.concatenate([b_e, k_e], axis=0).astype(BF16)
    same_head = (_iota2((LANE, 1), 0) // N_C) == (_iota2((1, LANE), 1) // N_C)
    y_ref[...] = y0 * jnp.exp(w_tot) + jnp.where(same_head, _dot_tn(uv, bk), 0.0)


def _rwkv_mix_kernel(*refs, c):
    (r_ref, lw_ref, k_ref, v_ref, kkr_ref, a_ref, g_ref, s0_ref, rk_ref, lnw_ref, lnb_ref,
     tri_ref, seg_ref, o_ref, s_out_ref, y_scr) = refs[:16]
    pads = refs[16:]
    step = pl.program_id(1)
    n_pair = H_C // 2

    @pl.when(step == 0)
    def _():
        y_scr[...] = jnp.zeros(y_scr.shape, F32)
        for p in range(n_pair):
            y_scr[p, 0:N_C, 0:N_C] = s0_ref[0, 2 * p]
            y_scr[p, N_C:LANE, N_C:LANE] = s0_ref[0, 2 * p + 1]

    ins = (r_ref, lw_ref, k_ref, v_ref, kkr_ref, a_ref, g_ref)
    if pads:
        r, lw, k, v, kk, a, g = [_load_rows(x, s, c) for x, s in zip(ins, pads)]
    else:
        r, lw, k, v, kk, a, g = [x[0] for x in ins]
    t_out = o_ref.shape[1]
    seg = seg_ref[...]

    kk = kk * lax.rsqrt(jnp.maximum(_segsum(kk * kk, seg), KK_EPS))
    cw = _sum01_left(tri_ref[...], lw)
    w_tot = cw[c - 1:c]
    kka = kk * a
    a_t = kk * jnp.exp(cw - lw)
    r_t = r * jnp.exp(cw)
    e_inv = jnp.exp(-cw)
    b_t = kka * e_inv
    k_t = k * e_inv
    e_end = jnp.exp(w_tot - cw)
    b_e = kka * e_end
    k_e = k * e_end

    outs = [[] for _ in range(n_pair)]
    gens = []
    for p in range(n_pair):
        sl = slice(p * LANE, (p + 1) * LANE)
        gens.append(_rwkv_pair(a_t[:, sl], r_t[:, sl], b_t[:, sl], k_t[:, sl], b_e[:, sl],
                               k_e[:, sl], v[:, sl], w_tot[:, sl], y_scr.at[p], outs[p], c))
    _lockstep(gens)
    o = jnp.concatenate([outs[p][0] for p in range(n_pair)], axis=1)

    mean = _segsum(o, seg) * (1.0 / N_C)
    var = _segsum(jnp.square(o - mean), seg) * (1.0 / N_C)
    on = (o - mean) * lax.rsqrt(var + GN_EPS) * lnw_ref[...] + lnb_ref[...]
    bonus = _segsum(r * k * rk_ref[...], seg) * v
    o_ref[0] = ((on + bonus) * g)[0:t_out]

    @pl.when(step == pl.num_programs(1) - 1)
    def _():
        for p in range(n_pair):
            s_out_ref[0, 2 * p] = y_scr[p, 0:N_C, 0:N_C]
            s_out_ref[0, 2 * p + 1] = y_scr[p, N_C:LANE, N_C:LANE]


def _rwkv_mix(pre, s0, rk, lnw, lnb, bsz, t_len):
    c = CHUNK
    assert 2 * c == LANE and c // SUB == 4
    tb = min(t_len, c)
    nc = -(-t_len // c)
    assert t_len == nc * tb
    pre = [a.reshape(bsz, t_len, D_MODEL) for a in pre]
    tri = jnp.asarray(np.tril(np.ones((c, c), np.float32)), BF16)
    seg = jnp.asarray(np.kron(np.eye(SEG // N_C, dtype=np.float32),
                              np.ones((N_C, N_C), np.float32)), BF16)
    row = pl.BlockSpec((1, tb, D_MODEL), lambda b, s: (b, s, 0))
    st = pl.BlockSpec((1,) + s0.shape[1:], lambda b, s: (b, 0, 0, 0))
    scratch = [pltpu.VMEM((H_C // 2, LANE, LANE), F32)]
    if tb < c:
        scratch += [pltpu.VMEM((c, D_MODEL), F32)] * 7
    o, s_new = pl.pallas_call(
        functools.partial(_rwkv_mix_kernel, c=c),
        out_shape=[jax.ShapeDtypeStruct((bsz, t_len, D_MODEL), F32),
                   jax.ShapeDtypeStruct(s0.shape, F32)],
        grid=(bsz, nc),
        in_specs=[row] * 7 + [st, _const_spec(rk.shape), _const_spec(lnw.shape),
                              _const_spec(lnb.shape), _const_spec(tri.shape),
                              _const_spec(seg.shape)],
        out_specs=[row, st],
        scratch_shapes=scratch,
        compiler_params=_cparams(("parallel", "arbitrary")),
        name="rwkv_mix",
    )(*pre, s0, rk, lnw, lnb, tri, seg)
    return o.reshape(bsz * t_len, D_MODEL), s_new


def _post_kernel(h_ref, o_ref, wo_ref, gf_ref, wg_ref, wu_ref, wd_ref, gp_ref, wpg_ref,
                 p_ref, wple_ref, gfin_ref, out_ref, *, final):
    h1 = h_ref[...] + _dot(o_ref[...].astype(BF16), wo_ref[...])
    x2 = _rms(h1, gf_ref[...]).astype(BF16)
    h2 = h1
    for lo in range(0, FFN_DIM, FFN_SPLIT):
        gate = _dot(x2, wg_ref[:, lo:lo + FFN_SPLIT])
        up = _dot(x2, wu_ref[:, lo:lo + FFN_SPLIT])
        act = (_silu(gate) * up).astype(BF16)
        h2 = h2 + _dot(act, wd_ref[lo:lo + FFN_SPLIT, :])
    x3 = _rms(h2, gp_ref[...]).astype(BF16)
    gate = _sigmoid(_dot(x3, wpg_ref[...]))
    emb = _dot(p_ref[...].astype(BF16), wple_ref[...])
    h3 = h2 + gate * emb
    out_ref[...] = _rms(h3, gfin_ref[...]) if final else h3


def _post(h, o, p, wts, final, tm):
    m = h.shape[0]
    wo, gf, wg, wu, wd, gp, wpg, wple, gfin = wts
    row = lambda w: pl.BlockSpec((tm, w), lambda i: (i, 0))
    return pl.pallas_call(
        functools.partial(_post_kernel, final=final),
        out_shape=jax.ShapeDtypeStruct((m, D_MODEL), F32),
        grid=(m // tm,),
        in_specs=[row(D_MODEL), row(D_MODEL), _const_spec(wo.shape), _const_spec(gf.shape),
                  _const_spec(wg.shape), _const_spec(wu.shape), _const_spec(wd.shape),
                  _const_spec(gp.shape), _const_spec(wpg.shape), row(PLE_DIM),
                  _const_spec(wple.shape), _const_spec(gfin.shape)],
        out_specs=row(D_MODEL),
        compiler_params=_cparams(("parallel",)),
        name="post",
    )(h, o, wo, gf, wg, wu, wd, gp, wpg, p, wple, gfin)


def _run_group(x, p, s_gla, s_hgrn, s_rwkv, s_shift, wt):
    bsz, t_len, _ = x.shape
    m = bsz * t_len
    tm = min(512, m)
    tm_rw = min(256, m)
    h = x.reshape(m, D_MODEL)
    out_gla, out_hgrn, out_rwkv, out_shift = [], [], [], []
    v_first = None
    row1 = lambda v: v.reshape(1, -1)
    for i in range(DEPTH):
        j = i // 2
        gm = row1(wt['norm_mix'][i])
        if i % 2 == 0:
            pre = _even_pre(h, gm, wt['w_cat'][j], wt['w_gk1'][j], wt['w_gk2'][j],
                            row1(wt['b_gk'][j]), wt['hgrn_gamma'], j, tm)
            o, sg, sh = _even_mix(pre, s_gla[j], s_hgrn[j], row1(wt['gla_norm'][j]),
                                  row1(wt['hgrn_norm'][j]), bsz, t_len)
            out_gla.append(sg)
            out_hgrn.append(sh)
            w_o = wt['w_out_even'][j]
        else:
            xl = jnp.broadcast_to(s_shift[j][:, None, :], (bsz, t_len, D_MODEL)).reshape(m, D_MODEL)
            wts = [gm, wt['rw_mu'][j], wt['rw_wr'][j], wt['rw_wk'][j], wt['rw_wv'][j],
                   row1(wt['rw_w0'][j]), wt['rw_w1'][j], wt['rw_w2'][j],
                   row1(wt['rw_a0'][j]), wt['rw_a1'][j], wt['rw_a2'][j],
                   wt['rw_g1'][j], wt['rw_g2'][j], row1(wt['rw_kk'][j]), row1(wt['rw_ka'][j])]
            vmix = None
            if j > 0:
                vmix = [v_first, row1(wt['rw_v0'][j - 1]), wt['rw_v1'][j - 1], wt['rw_v2'][j - 1]]
            r, lw, k, v, kkr, a, g, xn = _rwkv_pre(h, xl, t_len, wts, vmix, tm_rw)
            if j == 0:
                v_first = v
            o, sr = _rwkv_mix((r, lw, k, v, kkr, a, g), s_rwkv[j], row1(wt['rw_rk'][j]),
                              row1(wt['rw_lnw'][j]), row1(wt['rw_lnb'][j]), bsz, t_len)
            out_rwkv.append(sr)
            out_shift.append(xn.reshape(bsz, t_len, D_MODEL)[:, -1])
            w_o = wt['rw_wo'][j]
        post_w = [w_o, row1(wt['norm_ffn'][i]), wt['w_ffn_gate'][i], wt['w_ffn_up'][i],
                  wt['w_ffn_down'][i], row1(wt['norm_ple'][i]), wt['w_ple_gate'][i],
                  wt['w_ple'][i], row1(wt['norm_final'])]
        h = _post(h, o, p[i].reshape(m, PLE_DIM), post_w, i == DEPTH - 1, tm)
    return (h.reshape(bsz, t_len, D_MODEL), jnp.stack(out_gla), jnp.stack(out_hgrn),
            jnp.stack(out_rwkv), jnp.stack(out_shift))


def kernel(x_prompt, x_sample, p_prompt, p_sample, state_gla, state_hgrn, state_rwkv, state_shift, norm_mix, norm_ffn, norm_ple, norm_final, w_in_even, w_gk2, b_gk, gla_norm, hgrn_gamma, hgrn_norm, w_out_even, rw_mu, rw_wr, rw_wk, rw_wv, rw_wo, rw_w0, rw_w1, rw_w2, rw_a0, rw_a1, rw_a2, rw_v0, rw_v1, rw_v2, rw_g1, rw_g2, rw_kk, rw_ka, rw_rk, rw_lnw, rw_lnb, w_ffn_gate, w_ffn_up, w_ffn_down, w_ple, w_ple_gate):
    bf = lambda w: w.astype(BF16)
    w_cat = jnp.concatenate([w_in_even[:, :, 0:1536], w_in_even[:, :, 1552:3600]], axis=-1)
    w_gk1 = jnp.pad(w_in_even[:, :, 1536:1552], ((0, 0), (0, 0), (0, LANE - GK_RANK)))
    w_gk2p = jnp.pad(w_gk2, ((0, 0), (0, LANE - GK_RANK), (0, 0)))
    pad_c = lambda w: jnp.pad(w, ((0, 0), (0, 0), (0, LANE - w.shape[2])))
    pad_r = lambda w: jnp.pad(w, ((0, 0), (0, LANE - w.shape[1]), (0, 0)))
    wt = dict(
        norm_mix=norm_mix, norm_ffn=norm_ffn, norm_ple=norm_ple, norm_final=norm_final,
        w_cat=bf(w_cat), w_gk1=bf(w_gk1), w_gk2=bf(w_gk2p), b_gk=b_gk, gla_norm=gla_norm,
        hgrn_gamma=hgrn_gamma, hgrn_norm=hgrn_norm, w_out_even=bf(w_out_even),
        rw_mu=rw_mu, rw_wr=bf(rw_wr), rw_wk=bf(rw_wk), rw_wv=bf(rw_wv), rw_wo=bf(rw_wo),
        rw_w0=rw_w0, rw_w1=bf(pad_c(rw_w1)), rw_w2=bf(pad_r(rw_w2)),
        rw_a0=rw_a0, rw_a1=bf(pad_c(rw_a1)), rw_a2=bf(pad_r(rw_a2)),
        rw_v0=rw_v0, rw_v1=bf(pad_c(rw_v1)), rw_v2=bf(pad_r(rw_v2)),
        rw_g1=bf(rw_g1), rw_g2=bf(rw_g2), rw_kk=rw_kk, rw_ka=rw_ka, rw_rk=rw_rk,
        rw_lnw=rw_lnw, rw_lnb=rw_lnb,
        w_ffn_gate=bf(w_ffn_gate), w_ffn_up=bf(w_ffn_up), w_ffn_down=bf(w_ffn_down),
        w_ple=bf(w_ple), w_ple_gate=bf(w_ple_gate),
    )
    bp = x_prompt.shape[0]
    zeros = lambda s: jnp.zeros((s.shape[0], bp) + s.shape[2:], F32)
    y_p, gla_p, hgrn_p, rwkv_p, shift_p = _run_group(
        x_prompt, p_prompt, zeros(state_gla), zeros(state_hgrn), zeros(state_rwkv),
        zeros(state_shift), wt)
    y_s, gla_s, hgrn_s, rwkv_s, shift_s = _run_group(
        x_sample, p_sample, state_gla, state_hgrn, state_rwkv, state_shift, wt)
    return (y_p, y_s, gla_p, hgrn_p, rwkv_p, shift_p, gla_s, hgrn_s, rwkv_s, shift_s)
```

```python
import functools
import math

import numpy as np
import jax
import jax.numpy as jnp
from jax import lax
from jax.experimental import pallas as pl
from jax.experimental.pallas import tpu as pltpu

F32 = jnp.float32
BF16 = jnp.bfloat16

D_MODEL = 1024
DEPTH = 4
MIX_A = 512
H_A = 4
DV_A = 128
DK_A = 64
GK_RANK = 16
GK_NORM = 16.0
H_B = 4
DV_B = 128
N_B = 128
LB_FLOOR = 1e-20
N_C = 64
H_C = D_MODEL // N_C
FFN_DIM = 2816
PLE_DIM = 256
RMS_EPS = 1e-6
GN_EPS = 64e-5
KK_EPS = 1e-24

LANE = 128
CHUNK = 64
SUB = 16
VMEM_LIMIT = 56 * 1024 * 1024
FFN_SPLIT = 1408
PAIR_V = 2 * DV_A
SEG = 256
SEQ_ROWS = 16

EVEN_WIDTHS = (256, 256, 512, 512, 256, 512, 512, 512, 512, 512)
EVEN_DTYPES = (BF16, BF16, BF16, BF16, F32, BF16, BF16, BF16, BF16, F32)
RWKV_DTYPES = (BF16, F32, BF16, BF16, BF16, BF16, BF16, F32)


def _cparams(sem):
    return pltpu.CompilerParams(dimension_semantics=sem, vmem_limit_bytes=VMEM_LIMIT)


def _const_spec(shape):
    nd = len(shape)
    return pl.BlockSpec(shape, lambda *_: (0,) * nd, pipeline_mode=pl.Buffered(1))


def _rms(x, g, eps=RMS_EPS):
    return x * lax.rsqrt(jnp.mean(x * x, axis=-1, keepdims=True) + eps) * g


def _sigmoid(x):
    return 1.0 / (1.0 + jnp.exp(-x))


def _silu(x):
    return x * _sigmoid(x)


def _log_sigmoid(x):
    return jnp.minimum(x, 0.0) - jnp.log1p(jnp.exp(-jnp.abs(x)))


def _dot(a, b):
    return jnp.dot(a, b, preferred_element_type=F32)


def _dot_nt(a, b):
    return lax.dot_general(a, b, (((1,), (1,)), ((), ())), preferred_element_type=F32)


def _dot_tn(a, b):
    return lax.dot_general(a, b, (((0,), (0,)), ((), ())), preferred_element_type=F32)


def _mm(a, b):
    return _dot(a.astype(BF16), b.astype(BF16))


def _split2(x):
    hi = x.astype(BF16)
    return hi, (x - hi.astype(F32)).astype(BF16)


def _split3(x):
    hi = x.astype(BF16)
    r1 = x - hi.astype(F32)
    mid = r1.astype(BF16)
    return hi, mid, (r1 - mid.astype(F32)).astype(BF16)


def _sum01_left(m01, x):
    hi, mid, lo = _split3(x)
    return _dot(m01, hi) + _dot(m01, mid) + _dot(m01, lo)


def _sum01_right(x, m01):
    hi, mid, lo = _split3(x)
    return _dot(hi, m01) + _dot(mid, m01) + _dot(lo, m01)


def _iota2(shape, axis):
    return lax.broadcasted_iota(jnp.int32, shape, axis)


def _stack_heads(x, first):
    return jnp.concatenate([jnp.where(first, x, 0), jnp.where(first, 0, x)], axis=0)


def _lockstep(gens):
    gens = list(gens)
    while gens:
        alive = []
        for g in gens:
            try:
                next(g)
                alive.append(g)
            except StopIteration:
                pass
        gens = alive


def _load_rows(ref, scr, c):
    t = ref.shape[1]
    if t == c:
        return ref[0]
    scr[...] = jnp.zeros(scr.shape, F32)
    scr[0:t, :] = ref[0].astype(F32)
    return scr[...]


def _put(ref, val):
    ref[...] = val.astype(ref.dtype)


def _rows_to_cols(x, eye_rows):
    hi, mid, lo = _split3(x)
    return _dot_tn(hi, eye_rows) + _dot_tn(mid, eye_rows) + _dot_tn(lo, eye_rows)


def _even_pre_kernel(h_ref, gm_ref, w_ref, wgk1_ref, wgk2_ref, bgk_ref, gam_ref,
                     qa_ref, ka_ref, va_ref, ga_ref, lga_ref,
                     qb_ref, kb_ref, vb_ref, gb_ref, lgb_ref, *, layer):
    xn = _rms(h_ref[...], gm_ref[...]).astype(BF16)

    def proj(lo, hi):
        return _dot(xn, w_ref[:, lo:hi])

    _put(qa_ref, proj(0, 256) * DK_A ** -0.5)
    _put(ka_ref, proj(256, 512))
    _put(va_ref, proj(512, 1024))
    _put(ga_ref, _silu(proj(1024, 1536)))
    gk_lr = _dot(xn, wgk1_ref[...])
    gk = _mm(gk_lr, wgk2_ref[...]) + bgk_ref[...]
    _put(lga_ref, _log_sigmoid(gk) / GK_NORM)

    gam = gam_ref[...]
    n_even = gam.shape[0]
    gmax = gam[0:1]
    for i in range(1, n_even):
        gmax = jnp.maximum(gmax, gam[i:i + 1])
    es = [jnp.exp(gam[i:i + 1] - gmax) for i in range(n_even)]
    den = es[0]
    for i in range(1, n_even):
        den = den + es[i]
    sm = [e / den for e in es]
    cum = sm[0]
    for i in range(1, layer + 1):
        cum = cum + sm[i]
    lb = cum - sm[0]

    _put(qb_ref, _silu(proj(1536, 2048)) * N_B ** -0.5)
    z = proj(2048, 2560)
    _put(lgb_ref, jnp.log(jnp.maximum(lb, LB_FLOOR) + (1.0 - lb) * _sigmoid(z)))
    _put(kb_ref, (1.0 - lb) * _sigmoid(-z))
    _put(vb_ref, proj(2560, 3072))
    _put(gb_ref, _silu(proj(3072, 3584)))


def _even_pre(h, gm, w_cat, wgk1, wgk2, bgk, gamma, layer, tm):
    m = h.shape[0]
    row = lambda w: pl.BlockSpec((tm, w), lambda i: (i, 0))
    return pl.pallas_call(
        functools.partial(_even_pre_kernel, layer=layer),
        out_shape=[jax.ShapeDtypeStruct((m, w), d) for w, d in zip(EVEN_WIDTHS, EVEN_DTYPES)],
        grid=(m // tm,),
        in_specs=[row(D_MODEL), _const_spec(gm.shape), _const_spec(w_cat.shape),
                  _const_spec(wgk1.shape), _const_spec(wgk2.shape), _const_spec(bgk.shape),
                  _const_spec(gamma.shape)],
        out_specs=[row(w) for w in EVEN_WIDTHS],
        compiler_params=_cparams(("parallel",)),
        name="even_pre",
    )(h, gm, w_cat, wgk1, wgk2, bgk, gamma)


def _level_widths(c):
    return [c >> i for i in range(1, int(math.log2(c)) + 1)]


def _decay_sum_matrix(c):
    i = np.arange(c)[:, None]
    t = np.arange(c)[None, :]
    blocks = [t <= i, t > i]
    for w in _level_widths(c):
        ref = (i // (2 * w)) * (2 * w) + w
        blocks.append(np.where(i >= ref, (t > ref) & (t <= i), (t > i) & (t <= ref)))
    return np.concatenate(blocks, axis=0).astype(np.float32)


def _level_index(c):
    t = np.arange(c)[:, None]
    s = np.arange(c)[None, :]
    lev = np.full((c, c), -1, np.int32)
    lev[t == s] = 0
    for li, w in enumerate(_level_widths(c)):
        m = (t // (2 * w) == s // (2 * w)) & (t % (2 * w) >= w) & (s % (2 * w) < w)
        lev[m] = li + 1
    return np.tile(lev, (1, 2))


def _gated_pair(q, k, v, lg, e_all, lev, ones_cv, y_ref, out, c):
    kw = q.shape[1]
    nlev = len(_level_widths(c))
    first_k = _iota2((1, kw), 1) < kw // 2
    first_v = _iota2((1, PAIR_V), 1) < PAIR_V // 2
    parts = []
    for l in range(nlev + 1):
        if l == 0:
            q_l, k_l = q, k
        else:
            e = e_all[(l + 1) * c:(l + 2) * c]
            q_l, k_l = q * e, k * e
        parts.append(_dot_nt(q_l.astype(BF16), _stack_heads(k_l.astype(BF16), first_k)))
    y0 = y_ref[...]
    o_inter = _mm(q * e_all[0:c], y0)
    upd = _dot_tn((k * e_all[c:2 * c]).astype(BF16), v.astype(BF16))
    l_hi, l_mid, l_lo = _split3(lg)
    b_last = _dot_tn(l_hi, ones_cv) + _dot_tn(l_mid, ones_cv) + _dot_tn(l_lo, ones_cv)
    yield
    a_mat = jnp.where(lev == 0, parts[0], 0.0)
    for l in range(1, nlev + 1):
        a_mat = jnp.where(lev == l, parts[l], a_mat)
    o = o_inter + _dot(a_mat.astype(BF16), _stack_heads(v.astype(BF16), first_v))
    same_head = (_iota2((kw, 1), 0) // (kw // 2)) == (_iota2((1, PAIR_V), 1) // (PAIR_V // 2))
    y_ref[...] = jnp.exp(b_last) * y0 + jnp.where(same_head, upd, 0.0)
    out.append(o)


def _even_mix_kernel(qa_ref, ka_ref, va_ref, ga_ref, lga_ref, qb_ref, kb_ref, vb_ref, gb_ref,
                     lgb_ref, msum_ref, lev_ref, ones_ref, sg0_ref, sh0_ref, gn_ref, hn_ref,
                     o_ref, sg_ref, sh_ref, ya_scr, yb_scr, *pads, c):
    step = pl.program_id(1)
    ka_w = 2 * DK_A
    kb_w = 2 * N_B

    @pl.when(step == 0)
    def _():
        ya_scr[...] = jnp.zeros(ya_scr.shape, F32)
        yb_scr[...] = jnp.zeros(yb_scr.shape, F32)
        for p in range(H_A // 2):
            ya_scr[p, 0:DK_A, 0:DV_A] = sg0_ref[0, 2 * p]
            ya_scr[p, DK_A:ka_w, DV_A:PAIR_V] = sg0_ref[0, 2 * p + 1]
        for p in range(H_B // 2):
            yb_scr[p, 0:N_B, 0:DV_B] = sh0_ref[0, 2 * p]
            yb_scr[p, N_B:kb_w, DV_B:PAIR_V] = sh0_ref[0, 2 * p + 1]

    ins = (qa_ref, ka_ref, va_ref, ga_ref, lga_ref, qb_ref, kb_ref, vb_ref, gb_ref, lgb_ref)
    if pads:
        qa, ka, va, ga, lga, qb, kb, vb, gb, lgb = [_load_rows(r, s, c) for r, s in zip(ins, pads)]
    else:
        qa, ka, va, ga, lga, qb, kb, vb, gb, lgb = [r[0].astype(F32) for r in ins]
    t_out = o_ref.shape[1]

    msum = msum_ref[...]
    lev = lev_ref[...]
    ones_cv = ones_ref[...]
    ea = jnp.exp(_sum01_left(msum, lga))
    eb = jnp.exp(_sum01_left(msum, lgb))

    outs_a = [[] for _ in range(H_A // 2)]
    outs_b = [[] for _ in range(H_B // 2)]
    gens = []
    for p in range(H_A // 2):
        ks = slice(p * ka_w, (p + 1) * ka_w)
        vs = slice(p * PAIR_V, (p + 1) * PAIR_V)
        gens.append(_gated_pair(qa[:, ks], ka[:, ks], va[:, vs], lga[:, ks], ea[:, ks], lev,
                                ones_cv, ya_scr.at[p], outs_a[p], c))
    for p in range(H_B // 2):
        ks = slice(p * kb_w, (p + 1) * kb_w)
        vs = slice(p * PAIR_V, (p + 1) * PAIR_V)
        gens.append(_gated_pair(qb[:, ks], kb[:, ks], vb[:, vs], lgb[:, ks], eb[:, ks], lev,
                                ones_cv, yb_scr.at[p], outs_b[p], c))
    _lockstep(gens)

    for h in range(H_A):
        o = outs_a[h // 2][0][:, (h % 2) * DV_A:(h % 2 + 1) * DV_A]
        vs = slice(h * DV_A, (h + 1) * DV_A)
        o_ref[0, :, vs] = (_rms(o, gn_ref[...]) * ga[:, vs])[0:t_out]
    for h in range(H_B):
        o = outs_b[h // 2][0][:, (h % 2) * DV_B:(h % 2 + 1) * DV_B]
        vs = slice(MIX_A + h * DV_B, MIX_A + (h + 1) * DV_B)
        o_ref[0, :, vs] = (_rms(o, hn_ref[...]) * gb[:, h * DV_B:(h + 1) * DV_B])[0:t_out]

    @pl.when(step == pl.num_programs(1) - 1)
    def _():
        for p in range(H_A // 2):
            sg_ref[0, 2 * p] = ya_scr[p, 0:DK_A, 0:DV_A]
            sg_ref[0, 2 * p + 1] = ya_scr[p, DK_A:ka_w, DV_A:PAIR_V]
        for p in range(H_B // 2):
            sh_ref[0, 2 * p] = yb_scr[p, 0:N_B, 0:DV_B]
            sh_ref[0, 2 * p + 1] = yb_scr[p, N_B:kb_w, DV_B:PAIR_V]


def _even_mix(pre, sg0, sh0, gn, hn, bsz, t_len):
    if t_len % CHUNK == 0:
        c, tb = CHUNK, CHUNK
    else:
        c, tb = SEQ_ROWS, t_len
        assert t_len < c
    pre = [a.reshape(bsz, t_len, a.shape[-1]) for a in pre]
    msum = jnp.asarray(_decay_sum_matrix(c), BF16)
    lev = jnp.asarray(_level_index(c))
    ones_cv = jnp.ones((c, PAIR_V), BF16)
    row = lambda w: pl.BlockSpec((1, tb, w), lambda b, s: (b, s, 0))
    st = lambda shape: pl.BlockSpec((1,) + shape[1:], lambda b, s: (b, 0, 0, 0))
    scratch = [pltpu.VMEM((H_A // 2, 2 * DK_A, PAIR_V), F32),
               pltpu.VMEM((H_B // 2, 2 * N_B, PAIR_V), F32)]
    if tb < c:
        scratch += [pltpu.VMEM((c, w), F32) for w in EVEN_WIDTHS]
    o, sg, sh = pl.pallas_call(
        functools.partial(_even_mix_kernel, c=c),
        out_shape=[jax.ShapeDtypeStruct((bsz, t_len, D_MODEL), F32),
                   jax.ShapeDtypeStruct(sg0.shape, F32),
                   jax.ShapeDtypeStruct(sh0.shape, F32)],
        grid=(bsz, t_len // tb),
        in_specs=[row(w) for w in EVEN_WIDTHS] + [
            _const_spec(msum.shape), _const_spec(lev.shape), _const_spec(ones_cv.shape),
            st(sg0.shape), st(sh0.shape), _const_spec(gn.shape), _const_spec(hn.shape)],
        out_specs=[row(D_MODEL), st(sg0.shape), st(sh0.shape)],
        scratch_shapes=scratch,
        compiler_params=_cparams(("parallel", "arbitrary")),
        name="even_mix",
    )(*pre, msum, lev, ones_cv, sg0, sh0, gn, hn)
    return o.reshape(bsz * t_len, D_MODEL), sg, sh


def _rwkv_pre_kernel(*refs, t_len, has_vmix):
    (h_ref, xl_ref, gm_ref, mu_ref, wr_ref, wk_ref, wv_ref, w0_ref, w1_ref, w2_ref,
     a0_ref, a1_ref, a2_ref, g1_ref, g2_ref, kk_ref, ka_ref) = refs[:17]
    pos = 17
    if has_vmix:
        vf_ref, v0_ref, v1_ref, v2_ref = refs[pos:pos + 4]
        pos += 4
    (r_out, lw_out, k_out, v_out, kkr_out, a_out, g_out, xn_out, carry) = refs[pos:]

    tm = h_ref.shape[0]
    xn = _rms(h_ref[...], gm_ref[...])
    xn_out[...] = xn
    rows = _iota2((tm, 1), 0)
    t_idx = (pl.program_id(0) * tm + rows) & (t_len - 1)
    prev = pltpu.roll(xn, 1, 0)
    prev = jnp.where(rows == 0, carry[0:1, :], prev)
    prev = jnp.where(t_idx == 0, xl_ref[...], prev)
    carry[0:1, :] = xn[tm - 1:tm, :]
    dx = prev - xn

    def mixed(i):
        return (xn + dx * mu_ref[i:i + 1, :]).astype(BF16)

    _put(r_out, _dot(mixed(0), wr_ref[...]))
    dec = w0_ref[...] + _mm(jnp.tanh(_dot(mixed(1), w1_ref[...])), w2_ref[...])
    _put(lw_out, -_sigmoid(dec) * math.exp(-0.5))
    k = _dot(mixed(2), wk_ref[...])
    xv = mixed(3)
    v = _dot(xv, wv_ref[...])
    a = _sigmoid(a0_ref[...] + _mm(_dot(mixed(4), a1_ref[...]), a2_ref[...]))
    _put(a_out, a)
    _put(g_out, _mm(_sigmoid(_dot(mixed(5), g1_ref[...])), g2_ref[...]))
    _put(kkr_out, k * kk_ref[...])
    _put(k_out, k * (1.0 + (a - 1.0) * ka_ref[...]))
    if has_vmix:
        gate = _sigmoid(v0_ref[...] + _mm(_dot(xv, v1_ref[...]), v2_ref[...]))
        v = v + (vf_ref[...].astype(F32) - v) * gate
    _put(v_out, v)


def _rwkv_pre(h, xl, t_len, wts, vmix, tm):
    m = h.shape[0]
    assert t_len & (t_len - 1) == 0
    row = pl.BlockSpec((tm, D_MODEL), lambda i: (i, 0))
    args = [h, xl] + list(wts)
    specs = [row, row] + [_const_spec(w.shape) for w in wts]
    if vmix is not None:
        args += list(vmix)
        specs += [row] + [_const_spec(w.shape) for w in vmix[1:]]
    return pl.pallas_call(
        functools.partial(_rwkv_pre_kernel, t_len=t_len, has_vmix=vmix is not None),
        out_shape=[jax.ShapeDtypeStruct((m, D_MODEL), d) for d in RWKV_DTYPES],
        grid=(m // tm,),
        in_specs=specs,
        out_specs=[row] * 8,
        scratch_shapes=[pltpu.VMEM((8, D_MODEL), F32)],
        compiler_params=_cparams(("arbitrary",)),
        name="rwkv_pre",
    )(*args)


def _mm_pair(x, y, first, x_terms=2, y_terms=2):
    xh = x.astype(BF16)
    yh = y.astype(BF16)
    ybh = _stack_heads(yh, first)
    acc = _dot(xh, ybh)
    if y_terms == 2:
        acc = acc + _dot(xh, _stack_heads((y - yh.astype(F32)).astype(BF16), first))
    if x_terms == 2:
        acc = acc + _dot((x - xh.astype(F32)).astype(BF16), ybh)
    return acc


def _segsum(x, seg):
    c = x.shape[0]
    n = D_MODEL // SEG
    xs = jnp.concatenate([x[:, i * SEG:(i + 1) * SEG] for i in range(n)], axis=0)
    ys = _sum01_right(xs, seg)
    return jnp.concatenate([ys[i * c:(i + 1) * c] for i in range(n)], axis=1)


def _rwkv_head_out(o, r, k, v, g, rk_ref, lnw_ref, lnb_ref, seg):
    mean = _segsum(o, seg) * (1.0 / N_C)
    var = _segsum(jnp.square(o - mean), seg) * (1.0 / N_C)
    on = (o - mean) * lax.rsqrt(var + GN_EPS) * lnw_ref[...] + lnb_ref[...]
    bonus = _segsum(r * k * rk_ref[...], seg) * v
    return (on + bonus) * g


def _rwkv_pair(a_t, r_t, b_t, k_t, b_e, k_e, v, w_tot, y_ref, out, c):
    t_i = _iota2((c, LANE), 0)
    s_i = _iota2((c, LANE), 1) & (c - 1)
    first = _iota2((1, LANE), 1) < N_C
    strict = s_i < t_i
    incl = s_i <= t_i
    eye = (s_i == t_i).astype(F32)
    same_blk = (s_i // SUB) == (t_i // SUB)

    lhs = jnp.concatenate([a_t, r_t], axis=0).astype(BF16)
    rhs = jnp.concatenate([_stack_heads(b_t.astype(BF16), first),
                           _stack_heads(k_t.astype(BF16), first)], axis=0)
    m1 = _dot_nt(lhs, rhs)
    l_ab = jnp.where(strict, m1[0:c, 0:LANE], 0.0).astype(BF16).astype(F32)
    l_ak = jnp.where(strict, m1[0:c, LANE:2 * LANE], 0.0)
    m_rb = jnp.where(incl, m1[c:2 * c, 0:LANE], 0.0)
    m_rk = jnp.where(incl, m1[c:2 * c, LANE:2 * LANE], 0.0)
    y0 = y_ref[...]
    xs = _dot_nt(lhs, y0.astype(BF16))
    v_bd = _stack_heads(v.astype(BF16), first)
    rhs0 = xs[0:c] + _dot(l_ak.astype(BF16), v_bd)
    yield
    l_bd = jnp.where(same_blk, l_ab, 0.0)
    l_off = l_ab - l_bd
    l2 = _mm_pair(l_bd, l_bd, first, 1, 1)
    p = eye - l_bd
    yield
    both = _mm_pair(jnp.concatenate([p, l2], axis=0), l2, first)
    p = p + both[0:c]
    l4 = both[c:2 * c]
    yield
    both = _mm_pair(jnp.concatenate([p, l4], axis=0), l4, first)
    p = p + both[0:c]
    l8 = both[c:2 * c]
    yield
    p = p + _mm_pair(p, l8, first)
    yield
    n = _mm_pair(p, l_off, first, 2, 1)
    yield
    n2 = _mm_pair(n, n, first)
    q = eye - n
    yield
    q = q + _mm_pair(q, n2, first)
    yield
    w = _mm_pair(p, rhs0, first, 2, 1)
    yield
    u = -_mm_pair(q, w, first, 2, 1)
    yield
    mr = jnp.concatenate([m_rb, m_rk], axis=1).astype(BF16)
    uv_bd = jnp.concatenate([_stack_heads(u.astype(BF16), first), v_bd], axis=0)
    out.append(xs[c:2 * c] + _dot(mr, uv_bd))
    uv = jnp.concatenate([u, v], axis=0).astype(BF16)
    bk = jnp.concatenate([b_e, k_e], axis=0).astype(BF16)
    same_head = (_iota2((LANE, 1), 0) // N_C) == (_iota2((1, LANE), 1) // N_C)
    y_ref[...] = y0 * jnp.exp(w_tot) + jnp.where(same_head, _dot_tn(uv, bk), 0.0)


def _rwkv_mix_kernel(r_ref, lw_ref, k_ref, v_ref, kkr_ref, a_ref, g_ref, s0_ref, rk_ref, lnw_ref,
                     lnb_ref, tri_ref, seg_ref, o_ref, s_out_ref, y_scr, *, c):
    step = pl.program_id(1)
    n_pair = H_C // 2

    @pl.when(step == 0)
    def _():
        y_scr[...] = jnp.zeros(y_scr.shape, F32)
        for p in range(n_pair):
            y_scr[p, 0:N_C, 0:N_C] = s0_ref[0, 2 * p]
            y_scr[p, N_C:LANE, N_C:LANE] = s0_ref[0, 2 * p + 1]

    ins = (r_ref, lw_ref, k_ref, v_ref, kkr_ref, a_ref, g_ref)
    r, lw, k, v, kk, a, g = [x[0].astype(F32) for x in ins]
    seg = seg_ref[...]

    kk = kk * lax.rsqrt(jnp.maximum(_segsum(kk * kk, seg), KK_EPS))
    cw = _sum01_left(tri_ref[...], lw)
    w_tot = cw[c - 1:c]
    kka = kk * a
    a_t = kk * jnp.exp(cw - lw)
    r_t = r * jnp.exp(cw)
    e_inv = jnp.exp(-cw)
    b_t = kka * e_inv
    k_t = k * e_inv
    e_end = jnp.exp(w_tot - cw)
    b_e = kka * e_end
    k_e = k * e_end

    outs = [[] for _ in range(n_pair)]
    gens = []
    for p in range(n_pair):
        sl = slice(p * LANE, (p + 1) * LANE)
        gens.append(_rwkv_pair(a_t[:, sl], r_t[:, sl], b_t[:, sl], k_t[:, sl], b_e[:, sl],
                               k_e[:, sl], v[:, sl], w_tot[:, sl], y_scr.at[p], outs[p], c))
    _lockstep(gens)
    o = jnp.concatenate([outs[p][0] for p in range(n_pair)], axis=1)
    o_ref[0] = _rwkv_head_out(o, r, k, v, g, rk_ref, lnw_ref, lnb_ref, seg)

    @pl.when(step == pl.num_programs(1) - 1)
    def _():
        for p in range(n_pair):
            s_out_ref[0, 2 * p] = y_scr[p, 0:N_C, 0:N_C]
            s_out_ref[0, 2 * p + 1] = y_scr[p, N_C:LANE, N_C:LANE]


def _rwkv_mix(pre, s0, rk, lnw, lnb, bsz, t_len):
    c = CHUNK
    assert 2 * c == LANE and c // SUB == 4 and t_len % c == 0
    pre = [a.reshape(bsz, t_len, D_MODEL) for a in pre]
    tri = jnp.asarray(np.tril(np.ones((c, c), np.float32)), BF16)
    seg = jnp.asarray(np.kron(np.eye(SEG // N_C, dtype=np.float32),
                              np.ones((N_C, N_C), np.float32)), BF16)
    row = pl.BlockSpec((1, c, D_MODEL), lambda b, s: (b, s, 0))
    st = pl.BlockSpec((1,) + s0.shape[1:], lambda b, s: (b, 0, 0, 0))
    scratch = [pltpu.VMEM((H_C // 2, LANE, LANE), F32)]
    o, s_new = pl.pallas_call(
        functools.partial(_rwkv_mix_kernel, c=c),
        out_shape=[jax.ShapeDtypeStruct((bsz, t_len, D_MODEL), F32),
                   jax.ShapeDtypeStruct(s0.shape, F32)],
        grid=(bsz, t_len // c),
        in_specs=[row] * 7 + [st, _const_spec(rk.shape), _const_spec(lnw.shape),
                              _const_spec(lnb.shape), _const_spec(tri.shape),
                              _const_spec(seg.shape)],
        out_specs=[row, st],
        scratch_shapes=scratch,
        compiler_params=_cparams(("parallel", "arbitrary")),
        name="rwkv_mix",
    )(*pre, s0, rk, lnw, lnb, tri, seg)
    return o.reshape(bsz * t_len, D_MODEL), s_new


def _rwkv_seq_kernel(r_ref, lw_ref, k_ref, v_ref, kkr_ref, a_ref, g_ref, s0_ref, rk_ref, lnw_ref,
                     lnb_ref, seg_ref, seg2_ref, eye_ref, o_ref, s_out_ref, y_scr, o_scr, *pads,
                     tp):
    t_len = r_ref.shape[1]
    n_pair = H_C // 2
    ins = (r_ref, lw_ref, k_ref, v_ref, kkr_ref, a_ref, g_ref)
    r, lw, k, v, kk, a, g = [_load_rows(x, s, tp) for x, s in zip(ins, pads)]
    seg = seg_ref[...]
    seg2 = seg2_ref[...]
    eye_rows = eye_ref[...]
    kk = kk * lax.rsqrt(jnp.maximum(_segsum(kk * kk, seg), KK_EPS))
    kka = kk * a
    w = jnp.exp(lw)
    o_scr[...] = jnp.zeros(o_scr.shape, F32)
    for p in range(n_pair):
        y_scr[p * N_C:(p + 1) * N_C, 0:N_C] = s0_ref[0, 2 * p]
        y_scr[p * N_C:(p + 1) * N_C, N_C:LANE] = s0_ref[0, 2 * p + 1]
    s = y_scr[...]

    first = _iota2((1, LANE), 1) < N_C
    diag = _iota2((N_C, LANE), 0) == (_iota2((N_C, LANE), 1) & (N_C - 1))
    pair_lanes = [slice(p * LANE, (p + 1) * LANE) for p in range(n_pair)]
    v_cols = [_rows_to_cols(v[:, sl], eye_rows) for sl in pair_lanes]

    def rows_of(x, t):
        return jnp.concatenate([jnp.broadcast_to(x[t:t + 1, sl], (N_C, LANE)) for sl in pair_lanes],
                               axis=0)

    def head_sums(x):
        hi, lo = _split2(x)
        return _dot(hi, seg2) + _dot(lo, seg2)

    for t in range(t_len):
        s_kk = head_sums(s * rows_of(kk, t))
        v_col = jnp.concatenate([jnp.where(first, vc[0:N_C, t:t + 1], vc[N_C:LANE, t:t + 1])
                                 for vc in v_cols], axis=0)
        s = s * rows_of(w, t) - s_kk * rows_of(kka, t) + v_col * rows_of(k, t)
        o_b = head_sums(s * rows_of(r, t))
        for p in range(n_pair):
            o_p = jnp.where(diag, o_b[p * N_C:(p + 1) * N_C], 0.0)
            o_scr[t:t + 1, pair_lanes[p]] = jnp.sum(o_p, axis=0, keepdims=True)
    y_scr[...] = s

    o_ref[0] = _rwkv_head_out(o_scr[...], r, k, v, g, rk_ref, lnw_ref, lnb_ref, seg)[0:t_len]
    for p in range(n_pair):
        s_out_ref[0, 2 * p] = y_scr[p * N_C:(p + 1) * N_C, 0:N_C]
        s_out_ref[0, 2 * p + 1] = y_scr[p * N_C:(p + 1) * N_C, N_C:LANE]


def _rwkv_seq(pre, s0, rk, lnw, lnb, bsz, t_len):
    tp = SEQ_ROWS
    assert t_len <= tp and 2 * N_C == LANE
    pre = [a.reshape(bsz, t_len, D_MODEL) for a in pre]
    ones_h = np.ones((N_C, N_C), np.float32)
    seg = jnp.asarray(np.kron(np.eye(SEG // N_C, dtype=np.float32), ones_h), BF16)
    seg2 = jnp.asarray(np.kron(np.eye(2, dtype=np.float32), ones_h), BF16)
    eye_rows = jnp.asarray(np.eye(tp, LANE, dtype=np.float32), BF16)
    row = pl.BlockSpec((1, t_len, D_MODEL), lambda b: (b, 0, 0))
    st = pl.BlockSpec((1,) + s0.shape[1:], lambda b: (b, 0, 0, 0))
    o, s_new = pl.pallas_call(
        functools.partial(_rwkv_seq_kernel, tp=tp),
        out_shape=[jax.ShapeDtypeStruct((bsz, t_len, D_MODEL), F32),
                   jax.ShapeDtypeStruct(s0.shape, F32)],
        grid=(bsz,),
        in_specs=[row] * 7 + [st, _const_spec(rk.shape), _const_spec(lnw.shape),
                              _const_spec(lnb.shape), _const_spec(seg.shape),
                              _const_spec(seg2.shape), _const_spec(eye_rows.shape)],
        out_specs=[row, st],
        scratch_shapes=[pltpu.VMEM((H_C // 2 * N_C, LANE), F32), pltpu.VMEM((tp, D_MODEL), F32)]
        + [pltpu.VMEM((tp, D_MODEL), F32)] * 7,
        compiler_params=_cparams(("parallel",)),
        name="rwkv_seq",
    )(*pre, s0, rk, lnw, lnb, seg, seg2, eye_rows)
    return o.reshape(bsz * t_len, D_MODEL), s_new


def _post_kernel(h_ref, o_ref, wo_ref, gf_ref, wg_ref, wu_ref, wd_ref, gp_ref, wpg_ref,
                 p_ref, wple_ref, gfin_ref, out_ref, *, final):
    h1 = h_ref[...] + _dot(o_ref[...].astype(BF16), wo_ref[...])
    x2 = _rms(h1, gf_ref[...]).astype(BF16)
    h2 = h1
    for lo in range(0, FFN_DIM, FFN_SPLIT):
        gate = _dot(x2, wg_ref[:, lo:lo + FFN_SPLIT])
        up = _dot(x2, wu_ref[:, lo:lo + FFN_SPLIT])
        act = (_silu(gate) * up).astype(BF16)
        h2 = h2 + _dot(act, wd_ref[lo:lo + FFN_SPLIT, :])
    x3 = _rms(h2, gp_ref[...]).astype(BF16)
    gate = _sigmoid(_dot(x3, wpg_ref[...]))
    emb = _dot(p_ref[...].astype(BF16), wple_ref[...])
    h3 = h2 + gate * emb
    out_ref[...] = _rms(h3, gfin_ref[...]) if final else h3


def _post(h, o, p, wts, final, tm):
    m = h.shape[0]
    wo, gf, wg, wu, wd, gp, wpg, wple, gfin = wts
    row = lambda w: pl.BlockSpec((tm, w), lambda i: (i, 0))
    return pl.pallas_call(
        functools.partial(_post_kernel, final=final),
        out_shape=jax.ShapeDtypeStruct((m, D_MODEL), F32),
        grid=(m // tm,),
        in_specs=[row(D_MODEL), row(D_MODEL), _const_spec(wo.shape), _const_spec(gf.shape),
                  _const_spec(wg.shape), _const_spec(wu.shape), _const_spec(wd.shape),
                  _const_spec(gp.shape), _const_spec(wpg.shape), row(PLE_DIM),
                  _const_spec(wple.shape), _const_spec(gfin.shape)],
        out_specs=row(D_MODEL),
        compiler_params=_cparams(("parallel",)),
        name="post",
    )(h, o, wo, gf, wg, wu, wd, gp, wpg, p, wple, gfin)


def _run_group(x, p, s_gla, s_hgrn, s_rwkv, s_shift, wt):
    bsz, t_len, _ = x.shape
    m = bsz * t_len
    tm = min(512, m)
    tm_rw = min(512, m)
    h = x.reshape(m, D_MODEL)
    out_gla, out_hgrn, out_rwkv, out_shift = [], [], [], []
    v_first = None
    row1 = lambda v: v.reshape(1, -1)
    for i in range(DEPTH):
        j = i // 2
        gm = row1(wt['norm_mix'][i])
        if i % 2 == 0:
            pre = _even_pre(h, gm, wt['w_cat'][j], wt['w_gk1'][j], wt['w_gk2'][j],
                            row1(wt['b_gk'][j]), wt['hgrn_gamma'], j, tm)
            o, sg, sh = _even_mix(pre, s_gla[j], s_hgrn[j], row1(wt['gla_norm'][j]),
                                  row1(wt['hgrn_norm'][j]), bsz, t_len)
            out_gla.append(sg)
            out_hgrn.append(sh)
            w_o = wt['w_out_even'][j]
        else:
            xl = jnp.broadcast_to(s_shift[j][:, None, :], (bsz, t_len, D_MODEL)).reshape(m, D_MODEL)
            wts = [gm, wt['rw_mu'][j], wt['rw_wr'][j], wt['rw_wk'][j], wt['rw_wv'][j],
                   row1(wt['rw_w0'][j]), wt['rw_w1'][j], wt['rw_w2'][j],
                   row1(wt['rw_a0'][j]), wt['rw_a1'][j], wt['rw_a2'][j],
                   wt['rw_g1'][j], wt['rw_g2'][j], row1(wt['rw_kk'][j]), row1(wt['rw_ka'][j])]
            vmix = None
            if j > 0:
                vmix = [v_first, row1(wt['rw_v0'][j - 1]), wt['rw_v1'][j - 1], wt['rw_v2'][j - 1]]
            r, lw, k, v, kkr, a, g, xn = _rwkv_pre(h, xl, t_len, wts, vmix, tm_rw)
            if j == 0:
                v_first = v
            rwkv = _rwkv_mix if t_len % CHUNK == 0 else _rwkv_seq
            o, sr = rwkv((r, lw, k, v, kkr, a, g), s_rwkv[j], row1(wt['rw_rk'][j]),
                         row1(wt['rw_lnw'][j]), row1(wt['rw_lnb'][j]), bsz, t_len)
            out_rwkv.append(sr)
            out_shift.append(xn.reshape(bsz, t_len, D_MODEL)[:, -1])
            w_o = wt['rw_wo'][j]
        post_w = [w_o, row1(wt['norm_ffn'][i]), wt['w_ffn_gate'][i], wt['w_ffn_up'][i],
                  wt['w_ffn_down'][i], row1(wt['norm_ple'][i]), wt['w_ple_gate'][i],
                  wt['w_ple'][i], row1(wt['norm_final'])]
        h = _post(h, o, p[i].reshape(m, PLE_DIM), post_w, i == DEPTH - 1, tm)
    return (h.reshape(bsz, t_len, D_MODEL), jnp.stack(out_gla), jnp.stack(out_hgrn),
            jnp.stack(out_rwkv), jnp.stack(out_shift))


def kernel(x_prompt, x_sample, p_prompt, p_sample, state_gla, state_hgrn, state_rwkv, state_shift, norm_mix, norm_ffn, norm_ple, norm_final, w_in_even, w_gk2, b_gk, gla_norm, hgrn_gamma, hgrn_norm, w_out_even, rw_mu, rw_wr, rw_wk, rw_wv, rw_wo, rw_w0, rw_w1, rw_w2, rw_a0, rw_a1, rw_a2, rw_v0, rw_v1, rw_v2, rw_g1, rw_g2, rw_kk, rw_ka, rw_rk, rw_lnw, rw_lnb, w_ffn_gate, w_ffn_up, w_ffn_down, w_ple, w_ple_gate):
    bf = lambda w: w.astype(BF16)
    w_cat = jnp.concatenate([w_in_even[:, :, 0:1536], w_in_even[:, :, 1552:3600]], axis=-1)
    w_gk1 = jnp.pad(w_in_even[:, :, 1536:1552], ((0, 0), (0, 0), (0, LANE - GK_RANK)))
    w_gk2p = jnp.pad(w_gk2, ((0, 0), (0, LANE - GK_RANK), (0, 0)))
    pad_c = lambda w: jnp.pad(w, ((0, 0), (0, 0), (0, LANE - w.shape[2])))
    pad_r = lambda w: jnp.pad(w, ((0, 0), (0, LANE - w.shape[1]), (0, 0)))
    wt = dict(
        norm_mix=norm_mix, norm_ffn=norm_ffn, norm_ple=norm_ple, norm_final=norm_final,
        w_cat=bf(w_cat), w_gk1=bf(w_gk1), w_gk2=bf(w_gk2p), b_gk=b_gk, gla_norm=gla_norm,
        hgrn_gamma=hgrn_gamma, hgrn_norm=hgrn_norm, w_out_even=bf(w_out_even),
        rw_mu=rw_mu, rw_wr=bf(rw_wr), rw_wk=bf(rw_wk), rw_wv=bf(rw_wv), rw_wo=bf(rw_wo),
        rw_w0=rw_w0, rw_w1=bf(pad_c(rw_w1)), rw_w2=bf(pad_r(rw_w2)),
        rw_a0=rw_a0, rw_a1=bf(pad_c(rw_a1)), rw_a2=bf(pad_r(rw_a2)),
        rw_v0=rw_v0, rw_v1=bf(pad_c(rw_v1)), rw_v2=bf(pad_r(rw_v2)),
        rw_g1=bf(rw_g1), rw_g2=bf(rw_g2), rw_kk=rw_kk, rw_ka=rw_ka, rw_rk=rw_rk,
        rw_lnw=rw_lnw, rw_lnb=rw_lnb,
        w_ffn_gate=bf(w_ffn_gate), w_ffn_up=bf(w_ffn_up), w_ffn_down=bf(w_ffn_down),
        w_ple=bf(w_ple), w_ple_gate=bf(w_ple_gate),
    )
    bp = x_prompt.shape[0]
    zeros = lambda s: jnp.zeros((s.shape[0], bp) + s.shape[2:], F32)
    y_p, gla_p, hgrn_p, rwkv_p, shift_p = _run_group(
        x_prompt, p_prompt, zeros(state_gla), zeros(state_hgrn), zeros(state_rwkv),
        zeros(state_shift), wt)
    y_s, gla_s, hgrn_s, rwkv_s, shift_s = _run_group(
        x_sample, p_sample, state_gla, state_hgrn, state_rwkv, state_shift, wt)
    return (y_p, y_s, gla_p, hgrn_p, rwkv_p, shift_p, gla_s, hgrn_s, rwkv_s, shift_s)
```

```python
import functools
import math

import numpy as np
import jax
import jax.numpy as jnp
from jax import lax
from jax.experimental import pallas as pl
from jax.experimental.pallas import tpu as pltpu

F32 = jnp.float32
BF16 = jnp.bfloat16

D_MODEL = 1024
DEPTH = 4
MIX_A = 512
H_A = 4
DV_A = 128
DK_A = 64
GK_RANK = 16
GK_NORM = 16.0
H_B = 4
DV_B = 128
N_B = 128
LB_FLOOR = 1e-20
N_C = 64
H_C = D_MODEL // N_C
FFN_DIM = 2816
PLE_DIM = 256
RMS_EPS = 1e-6
GN_EPS = 64e-5
KK_EPS = 1e-24

LANE = 128
CHUNK = 64
CHUNKS_PER_STEP = 2
SUB = 16
VMEM_LIMIT = 56 * 1024 * 1024
FFN_SPLIT = 1408
PAIR_V = 2 * DV_A
SEG = 256
SEQ_ROWS = 16

EVEN_WIDTHS = (256, 256, 512, 512, 256, 512, 512, 512, 512, 512)
EVEN_DTYPES = (BF16, BF16, BF16, BF16, F32, BF16, BF16, BF16, BF16, F32)
RWKV_DTYPES = (BF16, F32, BF16, BF16, BF16, BF16, BF16, F32)


def _cparams(sem):
    return pltpu.CompilerParams(dimension_semantics=sem, vmem_limit_bytes=VMEM_LIMIT)


def _const_spec(shape):
    nd = len(shape)
    return pl.BlockSpec(shape, lambda *_: (0,) * nd, pipeline_mode=pl.Buffered(1))


def _pspec(p):
    if not isinstance(p, tuple):
        return _const_spec(p.shape)
    a, layer = p
    nd = a.ndim - 1
    return pl.BlockSpec((None,) + a.shape[1:], lambda *_: (layer,) + (0,) * nd,
                        pipeline_mode=pl.Buffered(1))


def _parg(p):
    return p[0] if isinstance(p, tuple) else p


def _rms(x, g, eps=RMS_EPS):
    return x * lax.rsqrt(jnp.mean(x * x, axis=-1, keepdims=True) + eps) * g


def _sigmoid(x):
    return 1.0 / (1.0 + jnp.exp(-x))


def _silu(x):
    return x * _sigmoid(x)


def _log_sigmoid(x):
    return jnp.minimum(x, 0.0) - jnp.log1p(jnp.exp(-jnp.abs(x)))


def _dot(a, b):
    return jnp.dot(a, b, preferred_element_type=F32)


def _dot_nt(a, b):
    return lax.dot_general(a, b, (((1,), (1,)), ((), ())), preferred_element_type=F32)


def _dot_tn(a, b):
    return lax.dot_general(a, b, (((0,), (0,)), ((), ())), preferred_element_type=F32)


def _mm(a, b):
    return _dot(a.astype(BF16), b.astype(BF16))


def _split2(x):
    hi = x.astype(BF16)
    return hi, (x - hi.astype(F32)).astype(BF16)


def _split3(x):
    hi = x.astype(BF16)
    r1 = x - hi.astype(F32)
    mid = r1.astype(BF16)
    return hi, mid, (r1 - mid.astype(F32)).astype(BF16)


def _sum01_left(m01, x):
    hi, mid, lo = _split3(x)
    return _dot(m01, hi) + _dot(m01, mid) + _dot(m01, lo)


def _sum01_right(x, m01):
    n = x.shape[0]
    y = _dot(jnp.concatenate(_split3(x), axis=0), m01)
    return y[0:n] + y[n:2 * n] + y[2 * n:3 * n]


def _iota2(shape, axis):
    return lax.broadcasted_iota(jnp.int32, shape, axis)


def _stack_heads(x, first):
    return jnp.concatenate([jnp.where(first, x, 0), jnp.where(first, 0, x)], axis=0)


def _lockstep(gens):
    gens = list(gens)
    while gens:
        alive = []
        for g in gens:
            try:
                next(g)
                alive.append(g)
            except StopIteration:
                pass
        gens = alive


def _load_rows(ref, scr, c):
    t = ref.shape[1]
    if t == c:
        return ref[0]
    scr[...] = jnp.zeros(scr.shape, F32)
    scr[0:t, :] = ref[0].astype(F32)
    return scr[...]


def _put(ref, val):
    ref[...] = val.astype(ref.dtype)


def _rows_to_cols(x, eye_rows):
    hi, mid, lo = _split3(x)
    return _dot_tn(hi, eye_rows) + _dot_tn(mid, eye_rows) + _dot_tn(lo, eye_rows)


def _even_pre_kernel(h_ref, gm_ref, w_ref, wgk1_ref, wgk2_ref, bgk_ref, gam_ref,
                     qa_ref, ka_ref, va_ref, ga_ref, lga_ref,
                     qb_ref, kb_ref, vb_ref, gb_ref, lgb_ref, *, layer):
    xn = _rms(h_ref[...], gm_ref[...]).astype(BF16)

    def proj(lo, hi):
        return _dot(xn, w_ref[:, lo:hi])

    _put(qa_ref, proj(0, 256) * DK_A ** -0.5)
    _put(ka_ref, proj(256, 512))
    _put(va_ref, proj(512, 1024))
    _put(ga_ref, _silu(proj(1024, 1536)))
    gk_lr = _dot(xn, wgk1_ref[...])
    gk = _mm(gk_lr, wgk2_ref[...]) + bgk_ref[...]
    _put(lga_ref, _log_sigmoid(gk) / GK_NORM)

    gam = gam_ref[...]
    n_even = gam.shape[0]
    gmax = gam[0:1]
    for i in range(1, n_even):
        gmax = jnp.maximum(gmax, gam[i:i + 1])
    es = [jnp.exp(gam[i:i + 1] - gmax) for i in range(n_even)]
    den = es[0]
    for i in range(1, n_even):
        den = den + es[i]
    sm = [e / den for e in es]
    cum = sm[0]
    for i in range(1, layer + 1):
        cum = cum + sm[i]
    lb = cum - sm[0]

    _put(qb_ref, _silu(proj(1536, 2048)) * N_B ** -0.5)
    z = proj(2048, 2560)
    _put(lgb_ref, jnp.log(jnp.maximum(lb, LB_FLOOR) + (1.0 - lb) * _sigmoid(z)))
    _put(kb_ref, (1.0 - lb) * _sigmoid(-z))
    _put(vb_ref, proj(2560, 3072))
    _put(gb_ref, _silu(proj(3072, 3584)))


def _even_pre(h, params, layer, tm):
    m = h.shape[0]
    row = lambda w: pl.BlockSpec((tm, w), lambda i: (i, 0))
    return pl.pallas_call(
        functools.partial(_even_pre_kernel, layer=layer),
        out_shape=[jax.ShapeDtypeStruct((m, w), d) for w, d in zip(EVEN_WIDTHS, EVEN_DTYPES)],
        grid=(m // tm,),
        in_specs=[row(D_MODEL)] + [_pspec(p) for p in params],
        out_specs=[row(w) for w in EVEN_WIDTHS],
        compiler_params=_cparams(("parallel",)),
        name="even_pre",
    )(h, *[_parg(p) for p in params])


def _level_widths(c):
    return [c >> i for i in range(1, int(math.log2(c)) + 1)]


def _decay_sum_matrix(c):
    i = np.arange(c)[:, None]
    t = np.arange(c)[None, :]
    blocks = [t <= i, t > i]
    for w in _level_widths(c):
        ref = (i // (2 * w)) * (2 * w) + w
        blocks.append(np.where(i >= ref, (t > ref) & (t <= i), (t > i) & (t <= ref)))
    return np.concatenate(blocks, axis=0).astype(np.float32)


def _level_index(c):
    t = np.arange(c)[:, None]
    s = np.arange(c)[None, :]
    lev = np.full((c, c), -1, np.int32)
    lev[t == s] = 0
    for li, w in enumerate(_level_widths(c)):
        m = (t // (2 * w) == s // (2 * w)) & (t % (2 * w) >= w) & (s % (2 * w) < w)
        lev[m] = li + 1
    return np.tile(lev, (1, 2))


def _gated_pair(q, k, v, lg, e_all, lev, ones_cv, y_ref, out, c):
    kw = q.shape[1]
    nlev = len(_level_widths(c))
    first_k = _iota2((1, kw), 1) < kw // 2
    first_v = _iota2((1, PAIR_V), 1) < PAIR_V // 2
    parts = []
    for l in range(nlev + 1):
        if l == 0:
            q_l, k_l = q, k
        else:
            e = e_all[(l + 1) * c:(l + 2) * c]
            q_l, k_l = q * e, k * e
        parts.append(_dot_nt(q_l.astype(BF16), _stack_heads(k_l.astype(BF16), first_k)))
    y0 = y_ref[...]
    o_inter = _mm(q * e_all[0:c], y0)
    upd = _dot_tn((k * e_all[c:2 * c]).astype(BF16), v.astype(BF16))
    l_hi, l_mid, l_lo = _split3(lg)
    b_last = _dot_tn(l_hi, ones_cv) + _dot_tn(l_mid, ones_cv) + _dot_tn(l_lo, ones_cv)
    yield
    a_mat = jnp.where(lev == 0, parts[0], 0.0)
    for l in range(1, nlev + 1):
        a_mat = jnp.where(lev == l, parts[l], a_mat)
    o = o_inter + _dot(a_mat.astype(BF16), _stack_heads(v.astype(BF16), first_v))
    same_head = (_iota2((kw, 1), 0) // (kw // 2)) == (_iota2((1, PAIR_V), 1) // (PAIR_V // 2))
    y_ref[...] = jnp.exp(b_last) * y0 + jnp.where(same_head, upd, 0.0)
    out.append(o)


def _even_mix_kernel(sg_acc_ref, sh_acc_ref, qa_ref, ka_ref, va_ref, ga_ref, lga_ref, qb_ref,
                     kb_ref, vb_ref, gb_ref, lgb_ref, msum_ref, lev_ref, ones_ref, sg0_ref,
                     sh0_ref, gn_ref, hn_ref, o_ref, sg_ref, sh_ref, ya_scr, yb_scr, *pads, c):
    del sg_acc_ref, sh_acc_ref
    step = pl.program_id(1)
    ka_w = 2 * DK_A
    kb_w = 2 * N_B

    @pl.when(step == 0)
    def _():
        ya_scr[...] = jnp.zeros(ya_scr.shape, F32)
        yb_scr[...] = jnp.zeros(yb_scr.shape, F32)
        for p in range(H_A // 2):
            ya_scr[p, 0:DK_A, 0:DV_A] = sg0_ref[0, 2 * p]
            ya_scr[p, DK_A:ka_w, DV_A:PAIR_V] = sg0_ref[0, 2 * p + 1]
        for p in range(H_B // 2):
            yb_scr[p, 0:N_B, 0:DV_B] = sh0_ref[0, 2 * p]
            yb_scr[p, N_B:kb_w, DV_B:PAIR_V] = sh0_ref[0, 2 * p + 1]

    ins = (qa_ref, ka_ref, va_ref, ga_ref, lga_ref, qb_ref, kb_ref, vb_ref, gb_ref, lgb_ref)
    msum = msum_ref[...]
    lev = lev_ref[...]
    ones_cv = ones_ref[...]
    rows = o_ref.shape[1]
    for r0 in range(0, rows, c):
        if pads:
            vals = [_load_rows(r, s, c) for r, s in zip(ins, pads)]
        else:
            vals = [r[0, r0:r0 + c, :].astype(F32) for r in ins]
        qa, ka, va, ga, lga, qb, kb, vb, gb, lgb = vals
        t_out = min(c, rows)
        ea = jnp.exp(_sum01_left(msum, lga))
        eb = jnp.exp(_sum01_left(msum, lgb))

        outs_a = [[] for _ in range(H_A // 2)]
        outs_b = [[] for _ in range(H_B // 2)]
        gens = []
        for p in range(H_A // 2):
            ks = slice(p * ka_w, (p + 1) * ka_w)
            vs = slice(p * PAIR_V, (p + 1) * PAIR_V)
            gens.append(_gated_pair(qa[:, ks], ka[:, ks], va[:, vs], lga[:, ks], ea[:, ks], lev,
                                    ones_cv, ya_scr.at[p], outs_a[p], c))
        for p in range(H_B // 2):
            ks = slice(p * kb_w, (p + 1) * kb_w)
            vs = slice(p * PAIR_V, (p + 1) * PAIR_V)
            gens.append(_gated_pair(qb[:, ks], kb[:, ks], vb[:, vs], lgb[:, ks], eb[:, ks], lev,
                                    ones_cv, yb_scr.at[p], outs_b[p], c))
        _lockstep(gens)

        for h in range(H_A):
            o = outs_a[h // 2][0][:, (h % 2) * DV_A:(h % 2 + 1) * DV_A]
            vs = slice(h * DV_A, (h + 1) * DV_A)
            o_ref[0, r0:r0 + t_out, vs] = (_rms(o, gn_ref[...]) * ga[:, vs])[0:t_out]
        for h in range(H_B):
            o = outs_b[h // 2][0][:, (h % 2) * DV_B:(h % 2 + 1) * DV_B]
            vs = slice(MIX_A + h * DV_B, MIX_A + (h + 1) * DV_B)
            res = _rms(o, hn_ref[...]) * gb[:, h * DV_B:(h + 1) * DV_B]
            o_ref[0, r0:r0 + t_out, vs] = res[0:t_out]

    @pl.when(step == pl.num_programs(1) - 1)
    def _():
        for p in range(H_A // 2):
            sg_ref[0, 2 * p] = ya_scr[p, 0:DK_A, 0:DV_A]
            sg_ref[0, 2 * p + 1] = ya_scr[p, DK_A:ka_w, DV_A:PAIR_V]
        for p in range(H_B // 2):
            sh_ref[0, 2 * p] = yb_scr[p, 0:N_B, 0:DV_B]
            sh_ref[0, 2 * p + 1] = yb_scr[p, N_B:kb_w, DV_B:PAIR_V]


def _layer_state_spec(shape, layer, grid_rank):
    tail = (0,) * (len(shape) - 2)
    if grid_rank == 2:
        return pl.BlockSpec((None, 1) + shape[2:], lambda b, s: (layer, b) + tail)
    return pl.BlockSpec((None, 1) + shape[2:], lambda b: (layer, b) + tail)


_ANY_SPEC = pl.BlockSpec(memory_space=pl.ANY)


def _even_mix(pre, sg_in, sh_in, sg_acc, sh_acc, layer, gn, hn, bsz, t_len):
    if t_len % CHUNK == 0:
        c, tb = CHUNK, CHUNK * CHUNKS_PER_STEP
        assert t_len % tb == 0
    else:
        c, tb = SEQ_ROWS, t_len
        assert t_len < c
    pre = [a.reshape(bsz, t_len, a.shape[-1]) for a in pre]
    msum = jnp.asarray(_decay_sum_matrix(c), BF16)
    lev = jnp.asarray(_level_index(c))
    ones_cv = jnp.ones((c, PAIR_V), BF16)
    row = lambda w: pl.BlockSpec((1, tb, w), lambda b, s: (b, s, 0))
    st = lambda a: _layer_state_spec(a.shape, layer, 2)
    scratch = [pltpu.VMEM((H_A // 2, 2 * DK_A, PAIR_V), F32),
               pltpu.VMEM((H_B // 2, 2 * N_B, PAIR_V), F32)]
    if tb < c:
        scratch += [pltpu.VMEM((c, w), F32) for w in EVEN_WIDTHS]
    o, sg, sh = pl.pallas_call(
        functools.partial(_even_mix_kernel, c=c),
        out_shape=[jax.ShapeDtypeStruct((bsz, t_len, D_MODEL), F32),
                   jax.ShapeDtypeStruct(sg_acc.shape, F32),
                   jax.ShapeDtypeStruct(sh_acc.shape, F32)],
        grid=(bsz, t_len // tb),
        in_specs=[_ANY_SPEC, _ANY_SPEC] + [row(w) for w in EVEN_WIDTHS] + [
            _const_spec(msum.shape), _const_spec(lev.shape), _const_spec(ones_cv.shape),
            st(sg_in), st(sh_in), _pspec(gn), _pspec(hn)],
        out_specs=[row(D_MODEL), st(sg_acc), st(sh_acc)],
        scratch_shapes=scratch,
        input_output_aliases={0: 1, 1: 2},
        compiler_params=_cparams(("parallel", "arbitrary")),
        name="even_mix",
    )(sg_acc, sh_acc, *pre, msum, lev, ones_cv, sg_in, sh_in, _parg(gn), _parg(hn))
    return o.reshape(bsz * t_len, D_MODEL), sg, sh


def _rwkv_pre_kernel(*refs, t_len, has_vmix):
    (h_ref, xl_ref, gm_ref, mu_ref, wr_ref, wk_ref, wv_ref, w0_ref, w1_ref, w2_ref,
     a0_ref, a1_ref, a2_ref, g1_ref, g2_ref, kk_ref, ka_ref) = refs[:17]
    pos = 17
    if has_vmix:
        vf_ref, v0_ref, v1_ref, v2_ref = refs[pos:pos + 4]
        pos += 4
    (r_out, lw_out, k_out, v_out, kkr_out, a_out, g_out, xn_out, carry) = refs[pos:]

    tm = h_ref.shape[0]
    xn = _rms(h_ref[...], gm_ref[...])
    xn_out[...] = xn
    rows = _iota2((tm, 1), 0)
    t_idx = (pl.program_id(0) * tm + rows) & (t_len - 1)
    prev = pltpu.roll(xn, 1, 0)
    prev = jnp.where(rows == 0, carry[0:1, :], prev)
    prev = jnp.where(t_idx == 0, xl_ref[...], prev)
    carry[0:1, :] = xn[tm - 1:tm, :]
    dx = prev - xn

    def mixed(i):
        return (xn + dx * mu_ref[i:i + 1, :]).astype(BF16)

    _put(r_out, _dot(mixed(0), wr_ref[...]))
    dec = w0_ref[...] + _mm(jnp.tanh(_dot(mixed(1), w1_ref[...])), w2_ref[...])
    _put(lw_out, -_sigmoid(dec) * math.exp(-0.5))
    k = _dot(mixed(2), wk_ref[...])
    xv = mixed(3)
    v = _dot(xv, wv_ref[...])
    a = _sigmoid(a0_ref[...] + _mm(_dot(mixed(4), a1_ref[...]), a2_ref[...]))
    _put(a_out, a)
    _put(g_out, _mm(_sigmoid(_dot(mixed(5), g1_ref[...])), g2_ref[...]))
    _put(kkr_out, k * kk_ref[...])
    _put(k_out, k * (1.0 + (a - 1.0) * ka_ref[...]))
    if has_vmix:
        gate = _sigmoid(v0_ref[...] + _mm(_dot(xv, v1_ref[...]), v2_ref[...]))
        v = v + (vf_ref[...].astype(F32) - v) * gate
    _put(v_out, v)


def _rwkv_pre(h, xl, t_len, wts, vmix, tm):
    m = h.shape[0]
    assert t_len & (t_len - 1) == 0
    row = pl.BlockSpec((tm, D_MODEL), lambda i: (i, 0))
    args = [h, xl] + [_parg(w) for w in wts]
    specs = [row, row] + [_pspec(w) for w in wts]
    if vmix is not None:
        args += [vmix[0]] + [_parg(w) for w in vmix[1:]]
        specs += [row] + [_pspec(w) for w in vmix[1:]]
    return pl.pallas_call(
        functools.partial(_rwkv_pre_kernel, t_len=t_len, has_vmix=vmix is not None),
        out_shape=[jax.ShapeDtypeStruct((m, D_MODEL), d) for d in RWKV_DTYPES],
        grid=(m // tm,),
        in_specs=specs,
        out_specs=[row] * 8,
        scratch_shapes=[pltpu.VMEM((8, D_MODEL), F32)],
        compiler_params=_cparams(("arbitrary",)),
        name="rwkv_pre",
    )(*args)


def _mm_pair(x, y, first, x_terms=2, y_terms=2):
    m = x.shape[0]
    xh = x.astype(BF16)
    yh = y.astype(BF16)
    ybh = _stack_heads(yh, first)
    if x_terms == 2:
        both = _dot(jnp.concatenate([xh, (x - xh.astype(F32)).astype(BF16)], axis=0), ybh)
        acc = both[0:m] + both[m:2 * m]
    else:
        acc = _dot(xh, ybh)
    if y_terms == 2:
        acc = acc + _dot(xh, _stack_heads((y - yh.astype(F32)).astype(BF16), first))
    return acc


def _segsum(x, seg):
    c = x.shape[0]
    n = D_MODEL // SEG
    xs = jnp.concatenate([x[:, i * SEG:(i + 1) * SEG] for i in range(n)], axis=0)
    ys = _sum01_right(xs, seg)
    return jnp.concatenate([ys[i * c:(i + 1) * c] for i in range(n)], axis=1)


def _rwkv_head_out(o, r, k, v, g, rk_ref, lnw_ref, lnb_ref, seg):
    mean = _segsum(o, seg) * (1.0 / N_C)
    var = _segsum(jnp.square(o - mean), seg) * (1.0 / N_C)
    on = (o - mean) * lax.rsqrt(var + GN_EPS) * lnw_ref[...] + lnb_ref[...]
    bonus = _segsum(r * k * rk_ref[...], seg) * v
    return (on + bonus) * g


def _rwkv_pair(a_t, r_t, b_t, k_t, b_e, k_e, v, w_tot, y_ref, out, c):
    t_i = _iota2((c, LANE), 0)
    s_i = _iota2((c, LANE), 1) & (c - 1)
    first = _iota2((1, LANE), 1) < N_C
    strict = s_i < t_i
    incl = s_i <= t_i
    eye = (s_i == t_i).astype(F32)
    same_blk = (s_i // SUB) == (t_i // SUB)

    lhs = jnp.concatenate([a_t, r_t], axis=0).astype(BF16)
    rhs = jnp.concatenate([_stack_heads(b_t.astype(BF16), first),
                           _stack_heads(k_t.astype(BF16), first)], axis=0)
    m1 = _dot_nt(lhs, rhs)
    l_ab = jnp.where(strict, m1[0:c, 0:LANE], 0.0).astype(BF16).astype(F32)
    l_ak = jnp.where(strict, m1[0:c, LANE:2 * LANE], 0.0)
    m_rb = jnp.where(incl, m1[c:2 * c, 0:LANE], 0.0)
    m_rk = jnp.where(incl, m1[c:2 * c, LANE:2 * LANE], 0.0)
    y0 = y_ref[...]
    xs = _dot_nt(lhs, y0.astype(BF16))
    v_bd = _stack_heads(v.astype(BF16), first)
    rhs0 = xs[0:c] + _dot(l_ak.astype(BF16), v_bd)
    yield
    l_bd = jnp.where(same_blk, l_ab, 0.0)
    l_off = l_ab - l_bd
    l2 = _mm_pair(l_bd, l_bd, first, 1, 1)
    p = eye - l_bd
    yield
    both = _mm_pair(jnp.concatenate([p, l2], axis=0), l2, first)
    p = p + both[0:c]
    l4 = both[c:2 * c]
    yield
    both = _mm_pair(jnp.concatenate([p, l4], axis=0), l4, first)
    p = p + both[0:c]
    l8 = both[c:2 * c]
    yield
    p = p + _mm_pair(p, l8, first)
    yield
    n = _mm_pair(p, l_off, first, 2, 1)
    yield
    n2 = _mm_pair(n, n, first)
    q = eye - n
    yield
    q = q + _mm_pair(q, n2, first)
    yield
    w = _mm_pair(p, rhs0, first, 2, 1)
    yield
    u = -_mm_pair(q, w, first, 2, 1)
    yield
    mr = jnp.concatenate([m_rb, m_rk], axis=1).astype(BF16)
    uv_bd = jnp.concatenate([_stack_heads(u.astype(BF16), first), v_bd], axis=0)
    out.append(xs[c:2 * c] + _dot(mr, uv_bd))
    uv = jnp.concatenate([u, v], axis=0).astype(BF16)
    bk = jnp.concatenate([b_e, k_e], axis=0).astype(BF16)
    same_head = (_iota2((LANE, 1), 0) // N_C) == (_iota2((1, LANE), 1) // N_C)
    y_ref[...] = y0 * jnp.exp(w_tot) + jnp.where(same_head, _dot_tn(uv, bk), 0.0)


def _rwkv_mix_kernel(s_acc_ref, r_ref, lw_ref, k_ref, v_ref, kkr_ref, a_ref, g_ref, s0_ref, rk_ref,
                     lnw_ref, lnb_ref, tri_ref, seg_ref, o_ref, s_out_ref, y_scr, *, c):
    del s_acc_ref
    step = pl.program_id(1)
    n_pair = H_C // 2

    @pl.when(step == 0)
    def _():
        y_scr[...] = jnp.zeros(y_scr.shape, F32)
        for p in range(n_pair):
            y_scr[p, 0:N_C, 0:N_C] = s0_ref[0, 2 * p]
            y_scr[p, N_C:LANE, N_C:LANE] = s0_ref[0, 2 * p + 1]

    ins = (r_ref, lw_ref, k_ref, v_ref, kkr_ref, a_ref, g_ref)
    seg = seg_ref[...]
    for r0 in range(0, o_ref.shape[1], c):
        r, lw, k, v, kk, a, g = [x[0, r0:r0 + c, :].astype(F32) for x in ins]
        kk = kk * lax.rsqrt(jnp.maximum(_segsum(kk * kk, seg), KK_EPS))
        cw = _sum01_left(tri_ref[...], lw)
        w_tot = cw[c - 1:c]
        kka = kk * a
        a_t = kk * jnp.exp(cw - lw)
        r_t = r * jnp.exp(cw)
        e_inv = jnp.exp(-cw)
        b_t = kka * e_inv
        k_t = k * e_inv
        e_end = jnp.exp(w_tot - cw)
        b_e = kka * e_end
        k_e = k * e_end

        outs = [[] for _ in range(n_pair)]
        gens = []
        for p in range(n_pair):
            sl = slice(p * LANE, (p + 1) * LANE)
            gens.append(_rwkv_pair(a_t[:, sl], r_t[:, sl], b_t[:, sl], k_t[:, sl], b_e[:, sl],
                                   k_e[:, sl], v[:, sl], w_tot[:, sl], y_scr.at[p], outs[p], c))
        _lockstep(gens)
        o = jnp.concatenate([outs[p][0] for p in range(n_pair)], axis=1)
        o_ref[0, r0:r0 + c, :] = _rwkv_head_out(o, r, k, v, g, rk_ref, lnw_ref, lnb_ref, seg)

    @pl.when(step == pl.num_programs(1) - 1)
    def _():
        for p in range(n_pair):
            s_out_ref[0, 2 * p] = y_scr[p, 0:N_C, 0:N_C]
            s_out_ref[0, 2 * p + 1] = y_scr[p, N_C:LANE, N_C:LANE]


def _rwkv_mix(pre, s_in, s_acc, layer, rk, lnw, lnb, bsz, t_len):
    c = CHUNK
    tb = c * CHUNKS_PER_STEP
    assert 2 * c == LANE and c // SUB == 4 and t_len % tb == 0
    pre = [a.reshape(bsz, t_len, D_MODEL) for a in pre]
    tri = jnp.asarray(np.tril(np.ones((c, c), np.float32)), BF16)
    seg = jnp.asarray(np.kron(np.eye(SEG // N_C, dtype=np.float32),
                              np.ones((N_C, N_C), np.float32)), BF16)
    row = pl.BlockSpec((1, tb, D_MODEL), lambda b, s: (b, s, 0))
    st = _layer_state_spec(s_in.shape, layer, 2)
    scratch = [pltpu.VMEM((H_C // 2, LANE, LANE), F32)]
    o, s_new = pl.pallas_call(
        functools.partial(_rwkv_mix_kernel, c=c),
        out_shape=[jax.ShapeDtypeStruct((bsz, t_len, D_MODEL), F32),
                   jax.ShapeDtypeStruct(s_acc.shape, F32)],
        grid=(bsz, t_len // tb),
        in_specs=[_ANY_SPEC] + [row] * 7 + [st, _pspec(rk), _pspec(lnw), _pspec(lnb),
                                            _const_spec(tri.shape), _const_spec(seg.shape)],
        out_specs=[row, st],
        scratch_shapes=scratch,
        input_output_aliases={0: 1},
        compiler_params=_cparams(("parallel", "arbitrary")),
        name="rwkv_mix",
    )(s_acc, *pre, s_in, _parg(rk), _parg(lnw), _parg(lnb), tri, seg)
    return o.reshape(bsz * t_len, D_MODEL), s_new


def _rwkv_seq_kernel(s_acc_ref, r_ref, lw_ref, k_ref, v_ref, kkr_ref, a_ref, g_ref, s0_ref, rk_ref,
                     lnw_ref, lnb_ref, seg_ref, seg2_ref, eye_ref, o_ref, s_out_ref, y_scr, o_scr,
                     *pads, tp):
    del s_acc_ref
    t_len = r_ref.shape[1]
    n_pair = H_C // 2
    ins = (r_ref, lw_ref, k_ref, v_ref, kkr_ref, a_ref, g_ref)
    r, lw, k, v, kk, a, g = [_load_rows(x, s, tp) for x, s in zip(ins, pads)]
    seg = seg_ref[...]
    seg2 = seg2_ref[...]
    eye_rows = eye_ref[...]
    kk = kk * lax.rsqrt(jnp.maximum(_segsum(kk * kk, seg), KK_EPS))
    kka = kk * a
    w = jnp.exp(lw)
    o_scr[...] = jnp.zeros(o_scr.shape, F32)
    for p in range(n_pair):
        y_scr[p * N_C:(p + 1) * N_C, 0:N_C] = s0_ref[0, 2 * p]
        y_scr[p * N_C:(p + 1) * N_C, N_C:LANE] = s0_ref[0, 2 * p + 1]
    s = y_scr[...]

    first = _iota2((1, LANE), 1) < N_C
    diag = _iota2((N_C, LANE), 0) == (_iota2((N_C, LANE), 1) & (N_C - 1))
    pair_lanes = [slice(p * LANE, (p + 1) * LANE) for p in range(n_pair)]
    v_cols = [_rows_to_cols(v[:, sl], eye_rows) for sl in pair_lanes]

    def rows_of(x, t):
        return jnp.concatenate([jnp.broadcast_to(x[t:t + 1, sl], (N_C, LANE)) for sl in pair_lanes],
                               axis=0)

    def head_sums(x):
        n = x.shape[0]
        y = _dot(jnp.concatenate(_split2(x), axis=0), seg2)
        return y[0:n] + y[n:2 * n]

    for t in range(t_len):
        s_kk = head_sums(s * rows_of(kk, t))
        v_col = jnp.concatenate([jnp.where(first, vc[0:N_C, t:t + 1], vc[N_C:LANE, t:t + 1])
                                 for vc in v_cols], axis=0)
        s = s * rows_of(w, t) - s_kk * rows_of(kka, t) + v_col * rows_of(k, t)
        o_b = head_sums(s * rows_of(r, t))
        for p in range(n_pair):
            o_p = jnp.where(diag, o_b[p * N_C:(p + 1) * N_C], 0.0)
            o_scr[t:t + 1, pair_lanes[p]] = jnp.sum(o_p, axis=0, keepdims=True)
    y_scr[...] = s

    o_ref[0] = _rwkv_head_out(o_scr[...], r, k, v, g, rk_ref, lnw_ref, lnb_ref, seg)[0:t_len]
    for p in range(n_pair):
        s_out_ref[0, 2 * p] = y_scr[p * N_C:(p + 1) * N_C, 0:N_C]
        s_out_ref[0, 2 * p + 1] = y_scr[p * N_C:(p + 1) * N_C, N_C:LANE]


def _rwkv_seq(pre, s_in, s_acc, layer, rk, lnw, lnb, bsz, t_len):
    tp = SEQ_ROWS
    assert t_len <= tp and 2 * N_C == LANE
    pre = [a.reshape(bsz, t_len, D_MODEL) for a in pre]
    ones_h = np.ones((N_C, N_C), np.float32)
    seg = jnp.asarray(np.kron(np.eye(SEG // N_C, dtype=np.float32), ones_h), BF16)
    seg2 = jnp.asarray(np.kron(np.eye(2, dtype=np.float32), ones_h), BF16)
    eye_rows = jnp.asarray(np.eye(tp, LANE, dtype=np.float32), BF16)
    row = pl.BlockSpec((1, t_len, D_MODEL), lambda b: (b, 0, 0))
    st = _layer_state_spec(s_in.shape, layer, 1)
    o, s_new = pl.pallas_call(
        functools.partial(_rwkv_seq_kernel, tp=tp),
        out_shape=[jax.ShapeDtypeStruct((bsz, t_len, D_MODEL), F32),
                   jax.ShapeDtypeStruct(s_acc.shape, F32)],
        grid=(bsz,),
        in_specs=[_ANY_SPEC] + [row] * 7 + [st, _pspec(rk), _pspec(lnw), _pspec(lnb),
                                            _const_spec(seg.shape), _const_spec(seg2.shape),
                                            _const_spec(eye_rows.shape)],
        out_specs=[row, st],
        scratch_shapes=[pltpu.VMEM((H_C // 2 * N_C, LANE), F32), pltpu.VMEM((tp, D_MODEL), F32)]
        + [pltpu.VMEM((tp, D_MODEL), F32)] * 7,
        input_output_aliases={0: 1},
        compiler_params=_cparams(("parallel",)),
        name="rwkv_seq",
    )(s_acc, *pre, s_in, _parg(rk), _parg(lnw), _parg(lnb), seg, seg2, eye_rows)
    return o.reshape(bsz * t_len, D_MODEL), s_new


def _post_kernel(h_ref, o_ref, wo_ref, gf_ref, wg_ref, wu_ref, wd_ref, gp_ref, wpg_ref,
                 p_ref, wple_ref, gfin_ref, out_ref, *, final):
    h1 = h_ref[...] + _dot(o_ref[...].astype(BF16), wo_ref[...])
    x2 = _rms(h1, gf_ref[...]).astype(BF16)
    h2 = h1
    for lo in range(0, FFN_DIM, FFN_SPLIT):
        gate = _dot(x2, wg_ref[:, lo:lo + FFN_SPLIT])
        up = _dot(x2, wu_ref[:, lo:lo + FFN_SPLIT])
        act = (_silu(gate) * up).astype(BF16)
        h2 = h2 + _dot(act, wd_ref[lo:lo + FFN_SPLIT, :])
    x3 = _rms(h2, gp_ref[...]).astype(BF16)
    gate = _sigmoid(_dot(x3, wpg_ref[...]))
    emb = _dot(p_ref[...].astype(BF16), wple_ref[...])
    h3 = h2 + gate * emb
    out_ref[...] = _rms(h3, gfin_ref[...]) if final else h3


def _post(h, o, p, layer, wts, final, tm):
    m = h.shape[0]
    wo, gf, wg, wu, wd, gp, wpg, wple, gfin = wts
    row = lambda w: pl.BlockSpec((tm, w), lambda i: (i, 0))
    p_spec = pl.BlockSpec((None, tm, PLE_DIM), lambda i: (layer, i, 0))
    return pl.pallas_call(
        functools.partial(_post_kernel, final=final),
        out_shape=jax.ShapeDtypeStruct((m, D_MODEL), F32),
        grid=(m // tm,),
        in_specs=[row(D_MODEL), row(D_MODEL), _pspec(wo), _pspec(gf), _pspec(wg), _pspec(wu),
                  _pspec(wd), _pspec(gp), _pspec(wpg), p_spec, _pspec(wple), _pspec(gfin)],
        out_specs=row(D_MODEL),
        compiler_params=_cparams(("parallel",)),
        name="post",
    )(h, o, *[_parg(w) for w in (wo, gf, wg, wu, wd, gp, wpg)], p, _parg(wple), _parg(gfin))


def _run_group(x, p, s_gla, s_hgrn, s_rwkv, s_shift, wt):
    bsz, t_len, _ = x.shape
    m = bsz * t_len
    tm = min(512, m)
    h = x.reshape(m, D_MODEL)
    p = p.reshape(DEPTH, m, PLE_DIM)
    acc_gla = jnp.zeros(s_gla.shape, F32)
    acc_hgrn = jnp.zeros(s_hgrn.shape, F32)
    acc_rwkv = jnp.zeros(s_rwkv.shape, F32)
    out_shift = []
    v_first = None
    par = lambda name, layer: (wt[name], layer)
    for i in range(DEPTH):
        j = i // 2
        gm = par('norm_mix', i)
        if i % 2 == 0:
            pre = _even_pre(h, [gm, par('w_cat', j), par('w_gk1', j), par('w_gk2', j),
                                par('b_gk', j), wt['hgrn_gamma']], j, tm)
            o, acc_gla, acc_hgrn = _even_mix(pre, s_gla, s_hgrn, acc_gla, acc_hgrn, j,
                                             par('gla_norm', j), par('hgrn_norm', j), bsz, t_len)
            w_o = par('w_out_even', j)
        else:
            xl = jnp.broadcast_to(s_shift[j][:, None, :], (bsz, t_len, D_MODEL)).reshape(m, D_MODEL)
            wts = [gm] + [par(n, j) for n in ('rw_mu', 'rw_wr', 'rw_wk', 'rw_wv', 'rw_w0', 'rw_w1',
                                               'rw_w2', 'rw_a0', 'rw_a1', 'rw_a2', 'rw_g1', 'rw_g2',
                                               'rw_kk', 'rw_ka')]
            vmix = None
            if j > 0:
                vmix = [v_first] + [par(n, j - 1) for n in ('rw_v0', 'rw_v1', 'rw_v2')]
            r, lw, k, v, kkr, a, g, xn = _rwkv_pre(h, xl, t_len, wts, vmix, tm)
            if j == 0:
                v_first = v
            rwkv = _rwkv_mix if t_len % CHUNK == 0 else _rwkv_seq
            o, acc_rwkv = rwkv((r, lw, k, v, kkr, a, g), s_rwkv, acc_rwkv, j, par('rw_rk', j),
                               par('rw_lnw', j), par('rw_lnb', j), bsz, t_len)
            out_shift.append(xn.reshape(bsz, t_len, D_MODEL)[:, -1])
            w_o = par('rw_wo', j)
        post_w = [w_o] + [par(n, i) for n in ('norm_ffn', 'w_ffn_gate', 'w_ffn_up', 'w_ffn_down',
                                              'norm_ple', 'w_ple_gate', 'w_ple')] + [wt['norm_final']]
        h = _post(h, o, p, i, post_w, i == DEPTH - 1, tm)
    return (h.reshape(bsz, t_len, D_MODEL), acc_gla, acc_hgrn, acc_rwkv, jnp.stack(out_shift))


def kernel(x_prompt, x_sample, p_prompt, p_sample, state_gla, state_hgrn, state_rwkv, state_shift, norm_mix, norm_ffn, norm_ple, norm_final, w_in_even, w_gk2, b_gk, gla_norm, hgrn_gamma, hgrn_norm, w_out_even, rw_mu, rw_wr, rw_wk, rw_wv, rw_wo, rw_w0, rw_w1, rw_w2, rw_a0, rw_a1, rw_a2, rw_v0, rw_v1, rw_v2, rw_g1, rw_g2, rw_kk, rw_ka, rw_rk, rw_lnw, rw_lnb, w_ffn_gate, w_ffn_up, w_ffn_down, w_ple, w_ple_gate):
    bf = lambda w: w.astype(BF16)
    w_cat = jnp.concatenate([w_in_even[:, :, 0:1536], w_in_even[:, :, 1552:3600]], axis=-1)
    w_gk1 = jnp.pad(w_in_even[:, :, 1536:1552], ((0, 0), (0, 0), (0, LANE - GK_RANK)))
    w_gk2p = jnp.pad(w_gk2, ((0, 0), (0, LANE - GK_RANK), (0, 0)))
    pad_c = lambda w: jnp.pad(w, ((0, 0), (0, 0), (0, LANE - w.shape[2])))
    pad_r = lambda w: jnp.pad(w, ((0, 0), (0, LANE - w.shape[1]), (0, 0)))
    vec = lambda w: w.reshape(w.shape[0], 1, -1)
    wt = dict(
        norm_mix=vec(norm_mix), norm_ffn=vec(norm_ffn), norm_ple=vec(norm_ple),
        norm_final=norm_final.reshape(1, -1),
        w_cat=bf(w_cat), w_gk1=bf(w_gk1), w_gk2=bf(w_gk2p), b_gk=vec(b_gk),
        gla_norm=vec(gla_norm), hgrn_gamma=hgrn_gamma, hgrn_norm=vec(hgrn_norm),
        w_out_even=bf(w_out_even),
        rw_mu=rw_mu, rw_wr=bf(rw_wr), rw_wk=bf(rw_wk), rw_wv=bf(rw_wv), rw_wo=bf(rw_wo),
        rw_w0=vec(rw_w0), rw_w1=bf(pad_c(rw_w1)), rw_w2=bf(pad_r(rw_w2)),
        rw_a0=vec(rw_a0), rw_a1=bf(pad_c(rw_a1)), rw_a2=bf(pad_r(rw_a2)),
        rw_v0=vec(rw_v0), rw_v1=bf(pad_c(rw_v1)), rw_v2=bf(pad_r(rw_v2)),
        rw_g1=bf(rw_g1), rw_g2=bf(rw_g2), rw_kk=vec(rw_kk), rw_ka=vec(rw_ka), rw_rk=vec(rw_rk),
        rw_lnw=vec(rw_lnw), rw_lnb=vec(rw_lnb),
        w_ffn_gate=bf(w_ffn_gate), w_ffn_up=bf(w_ffn_up), w_ffn_down=bf(w_ffn_down),
        w_ple=bf(w_ple), w_ple_gate=bf(w_ple_gate),
    )
    bp = x_prompt.shape[0]
    zeros = lambda s: jnp.zeros((s.shape[0], bp) + s.shape[2:], F32)
    y_p, gla_p, hgrn_p, rwkv_p, shift_p = _run_group(
        x_prompt, p_prompt, zeros(state_gla), zeros(state_hgrn), zeros(state_rwkv),
        zeros(state_shift), wt)
    y_s, gla_s, hgrn_s, rwkv_s, shift_s = _run_group(
        x_sample, p_sample, state_gla, state_hgrn, state_rwkv, state_shift, wt)
    return (y_p, y_s, gla_p, hgrn_p, rwkv_p, shift_p, gla_s, hgrn_s, rwkv_s, shift_s)
```

```python
import functools
import math

import numpy as np
import jax
import jax.numpy as jnp
from jax import lax
from jax.experimental import pallas as pl
from jax.experimental.pallas import tpu as pltpu

F32 = jnp.float32
BF16 = jnp.bfloat16

D_MODEL = 1024
DEPTH = 4
MIX_A = 512
H_A = 4
DV_A = 128
DK_A = 64
GK_RANK = 16
GK_NORM = 16.0
H_B = 4
DV_B = 128
N_B = 128
LB_FLOOR = 1e-20
N_C = 64
H_C = D_MODEL // N_C
FFN_DIM = 2816
PLE_DIM = 256
RMS_EPS = 1e-6
GN_EPS = 64e-5
KK_EPS = 1e-24

LANE = 128
CHUNK = 64
CHUNKS_PER_STEP = 2
SUB = 16
VMEM_LIMIT = 56 * 1024 * 1024
FFN_SPLIT = 1408
PAIR_V = 2 * DV_A
SEG = 256
SEQ_ROWS = 16

EVEN_WIDTHS = (256, 256, 512, 512, 256, 512, 512, 512, 512, 512)
EVEN_DTYPES = (BF16, BF16, BF16, BF16, F32, BF16, BF16, BF16, BF16, F32)
RWKV_DTYPES = (BF16, F32, BF16, BF16, BF16, BF16, BF16, F32)


def _cparams(sem):
    return pltpu.CompilerParams(dimension_semantics=sem, vmem_limit_bytes=VMEM_LIMIT)


def _const_spec(shape):
    nd = len(shape)
    return pl.BlockSpec(shape, lambda *_: (0,) * nd, pipeline_mode=pl.Buffered(1))


def _pspec(p):
    if not isinstance(p, tuple):
        return _const_spec(p.shape)
    a, layer = p
    nd = a.ndim - 1
    return pl.BlockSpec((None,) + a.shape[1:], lambda *_: (layer,) + (0,) * nd,
                        pipeline_mode=pl.Buffered(1))


def _parg(p):
    return p[0] if isinstance(p, tuple) else p


def _rms(x, g, eps=RMS_EPS):
    return x * lax.rsqrt(jnp.mean(x * x, axis=-1, keepdims=True) + eps) * g


def _sigmoid(x):
    return 1.0 / (1.0 + jnp.exp(-x))


def _silu(x):
    return x * _sigmoid(x)


def _log_sigmoid(x):
    return jnp.minimum(x, 0.0) - jnp.log1p(jnp.exp(-jnp.abs(x)))


def _dot(a, b):
    return jnp.dot(a, b, preferred_element_type=F32)


def _dot_nt(a, b):
    return lax.dot_general(a, b, (((1,), (1,)), ((), ())), preferred_element_type=F32)


def _dot_tn(a, b):
    return lax.dot_general(a, b, (((0,), (0,)), ((), ())), preferred_element_type=F32)


def _mm(a, b):
    return _dot(a.astype(BF16), b.astype(BF16))


def _split3(x):
    hi = x.astype(BF16)
    r1 = x - hi.astype(F32)
    mid = r1.astype(BF16)
    return hi, mid, (r1 - mid.astype(F32)).astype(BF16)


def _sum01_left(m01, x):
    hi, mid, lo = _split3(x)
    return _dot(m01, hi) + _dot(m01, mid) + _dot(m01, lo)


def _sum01_right(x, m01):
    n = x.shape[0]
    y = _dot(jnp.concatenate(_split3(x), axis=0), m01)
    return y[0:n] + y[n:2 * n] + y[2 * n:3 * n]


def _iota2(shape, axis):
    return lax.broadcasted_iota(jnp.int32, shape, axis)


def _stack_heads(x, first):
    return jnp.concatenate([jnp.where(first, x, 0), jnp.where(first, 0, x)], axis=0)


def _lockstep(gens):
    gens = list(gens)
    while gens:
        alive = []
        for g in gens:
            try:
                next(g)
                alive.append(g)
            except StopIteration:
                pass
        gens = alive


def _load_rows(ref, scr, c):
    t = ref.shape[1]
    if t == c:
        return ref[0]
    scr[...] = jnp.zeros(scr.shape, F32)
    scr[0:t, :] = ref[0].astype(F32)
    return scr[...]


def _put(ref, val):
    ref[...] = val.astype(ref.dtype)


def _even_pre_kernel(h_ref, gm_ref, w_ref, wgk1_ref, wgk2_ref, bgk_ref, gam_ref,
                     qa_ref, ka_ref, va_ref, ga_ref, lga_ref,
                     qb_ref, kb_ref, vb_ref, gb_ref, lgb_ref, *, layer):
    xn = _rms(h_ref[...], gm_ref[...]).astype(BF16)

    def proj(lo, hi):
        return _dot(xn, w_ref[:, lo:hi])

    _put(qa_ref, proj(0, 256) * DK_A ** -0.5)
    _put(ka_ref, proj(256, 512))
    _put(va_ref, proj(512, 1024))
    _put(ga_ref, _silu(proj(1024, 1536)))
    gk_lr = _dot(xn, wgk1_ref[...])
    gk = _mm(gk_lr, wgk2_ref[...]) + bgk_ref[...]
    _put(lga_ref, _log_sigmoid(gk) / GK_NORM)

    gam = gam_ref[...]
    n_even = gam.shape[0]
    gmax = gam[0:1]
    for i in range(1, n_even):
        gmax = jnp.maximum(gmax, gam[i:i + 1])
    es = [jnp.exp(gam[i:i + 1] - gmax) for i in range(n_even)]
    den = es[0]
    for i in range(1, n_even):
        den = den + es[i]
    sm = [e / den for e in es]
    cum = sm[0]
    for i in range(1, layer + 1):
        cum = cum + sm[i]
    lb = cum - sm[0]

    _put(qb_ref, _silu(proj(1536, 2048)) * N_B ** -0.5)
    z = proj(2048, 2560)
    _put(lgb_ref, jnp.log(jnp.maximum(lb, LB_FLOOR) + (1.0 - lb) * _sigmoid(z)))
    _put(kb_ref, (1.0 - lb) * _sigmoid(-z))
    _put(vb_ref, proj(2560, 3072))
    _put(gb_ref, _silu(proj(3072, 3584)))


def _even_pre(h, params, layer, tm):
    m = h.shape[0]
    row = lambda w: pl.BlockSpec((tm, w), lambda i: (i, 0))
    return pl.pallas_call(
        functools.partial(_even_pre_kernel, layer=layer),
        out_shape=[jax.ShapeDtypeStruct((m, w), d) for w, d in zip(EVEN_WIDTHS, EVEN_DTYPES)],
        grid=(m // tm,),
        in_specs=[row(D_MODEL)] + [_pspec(p) for p in params],
        out_specs=[row(w) for w in EVEN_WIDTHS],
        compiler_params=_cparams(("parallel",)),
        name="even_pre",
    )(h, *[_parg(p) for p in params])


def _level_widths(c):
    return [c >> i for i in range(1, int(math.log2(c)) + 1)]


def _decay_sum_matrix(c):
    i = np.arange(c)[:, None]
    t = np.arange(c)[None, :]
    blocks = [t <= i, t > i]
    for w in _level_widths(c):
        ref = (i // (2 * w)) * (2 * w) + w
        blocks.append(np.where(i >= ref, (t > ref) & (t <= i), (t > i) & (t <= ref)))
    return np.concatenate(blocks, axis=0).astype(np.float32)


def _level_index(c):
    t = np.arange(c)[:, None]
    s = np.arange(c)[None, :]
    lev = np.full((c, c), -1, np.int32)
    lev[t == s] = 0
    for li, w in enumerate(_level_widths(c)):
        m = (t // (2 * w) == s // (2 * w)) & (t % (2 * w) >= w) & (s % (2 * w) < w)
        lev[m] = li + 1
    return np.tile(lev, (1, 2))


def _gated_pair(q, k, v, lg, e_all, lev, ones_cv, y_ref, out, c):
    kw = q.shape[1]
    nlev = len(_level_widths(c))
    first_k = _iota2((1, kw), 1) < kw // 2
    first_v = _iota2((1, PAIR_V), 1) < PAIR_V // 2
    parts = []
    for l in range(nlev + 1):
        if l == 0:
            q_l, k_l = q, k
        else:
            e = e_all[(l + 1) * c:(l + 2) * c]
            q_l, k_l = q * e, k * e
        parts.append(_dot_nt(q_l.astype(BF16), _stack_heads(k_l.astype(BF16), first_k)))
    y0 = y_ref[...]
    o_inter = _mm(q * e_all[0:c], y0)
    upd = _dot_tn((k * e_all[c:2 * c]).astype(BF16), v.astype(BF16))
    l_hi, l_mid, l_lo = _split3(lg)
    b_last = _dot_tn(l_hi, ones_cv) + _dot_tn(l_mid, ones_cv) + _dot_tn(l_lo, ones_cv)
    yield
    a_mat = jnp.where(lev == 0, parts[0], 0.0)
    for l in range(1, nlev + 1):
        a_mat = jnp.where(lev == l, parts[l], a_mat)
    o = o_inter + _dot(a_mat.astype(BF16), _stack_heads(v.astype(BF16), first_v))
    same_head = (_iota2((kw, 1), 0) // (kw // 2)) == (_iota2((1, PAIR_V), 1) // (PAIR_V // 2))
    y_ref[...] = jnp.exp(b_last) * y0 + jnp.where(same_head, upd, 0.0)
    out.append(o)


def _even_mix_kernel(sg_acc_ref, sh_acc_ref, qa_ref, ka_ref, va_ref, ga_ref, lga_ref, qb_ref,
                     kb_ref, vb_ref, gb_ref, lgb_ref, msum_ref, lev_ref, ones_ref, sg0_ref,
                     sh0_ref, gn_ref, hn_ref, o_ref, sg_ref, sh_ref, ya_scr, yb_scr, *pads, c):
    del sg_acc_ref, sh_acc_ref
    step = pl.program_id(1)
    ka_w = 2 * DK_A
    kb_w = 2 * N_B

    @pl.when(step == 0)
    def _():
        ya_scr[...] = jnp.zeros(ya_scr.shape, F32)
        yb_scr[...] = jnp.zeros(yb_scr.shape, F32)
        for p in range(H_A // 2):
            ya_scr[p, 0:DK_A, 0:DV_A] = sg0_ref[0, 2 * p]
            ya_scr[p, DK_A:ka_w, DV_A:PAIR_V] = sg0_ref[0, 2 * p + 1]
        for p in range(H_B // 2):
            yb_scr[p, 0:N_B, 0:DV_B] = sh0_ref[0, 2 * p]
            yb_scr[p, N_B:kb_w, DV_B:PAIR_V] = sh0_ref[0, 2 * p + 1]

    ins = (qa_ref, ka_ref, va_ref, ga_ref, lga_ref, qb_ref, kb_ref, vb_ref, gb_ref, lgb_ref)
    msum = msum_ref[...]
    lev = lev_ref[...]
    ones_cv = ones_ref[...]
    rows = o_ref.shape[1]
    for r0 in range(0, rows, c):
        if pads:
            vals = [_load_rows(r, s, c) for r, s in zip(ins, pads)]
        else:
            vals = [r[0, r0:r0 + c, :].astype(F32) for r in ins]
        qa, ka, va, ga, lga, qb, kb, vb, gb, lgb = vals
        t_out = min(c, rows)
        ea = jnp.exp(_sum01_left(msum, lga))
        eb = jnp.exp(_sum01_left(msum, lgb))

        outs_a = [[] for _ in range(H_A // 2)]
        outs_b = [[] for _ in range(H_B // 2)]
        gens = []
        for p in range(H_A // 2):
            ks = slice(p * ka_w, (p + 1) * ka_w)
            vs = slice(p * PAIR_V, (p + 1) * PAIR_V)
            gens.append(_gated_pair(qa[:, ks], ka[:, ks], va[:, vs], lga[:, ks], ea[:, ks], lev,
                                    ones_cv, ya_scr.at[p], outs_a[p], c))
        for p in range(H_B // 2):
            ks = slice(p * kb_w, (p + 1) * kb_w)
            vs = slice(p * PAIR_V, (p + 1) * PAIR_V)
            gens.append(_gated_pair(qb[:, ks], kb[:, ks], vb[:, vs], lgb[:, ks], eb[:, ks], lev,
                                    ones_cv, yb_scr.at[p], outs_b[p], c))
        _lockstep(gens)

        for h in range(H_A):
            o = outs_a[h // 2][0][:, (h % 2) * DV_A:(h % 2 + 1) * DV_A]
            vs = slice(h * DV_A, (h + 1) * DV_A)
            o_ref[0, r0:r0 + t_out, vs] = (_rms(o, gn_ref[...]) * ga[:, vs])[0:t_out]
        for h in range(H_B):
            o = outs_b[h // 2][0][:, (h % 2) * DV_B:(h % 2 + 1) * DV_B]
            vs = slice(MIX_A + h * DV_B, MIX_A + (h + 1) * DV_B)
            res = _rms(o, hn_ref[...]) * gb[:, h * DV_B:(h + 1) * DV_B]
            o_ref[0, r0:r0 + t_out, vs] = res[0:t_out]

    @pl.when(step == pl.num_programs(1) - 1)
    def _():
        for p in range(H_A // 2):
            sg_ref[0, 2 * p] = ya_scr[p, 0:DK_A, 0:DV_A]
            sg_ref[0, 2 * p + 1] = ya_scr[p, DK_A:ka_w, DV_A:PAIR_V]
        for p in range(H_B // 2):
            sh_ref[0, 2 * p] = yb_scr[p, 0:N_B, 0:DV_B]
            sh_ref[0, 2 * p + 1] = yb_scr[p, N_B:kb_w, DV_B:PAIR_V]


def _layer_state_spec(shape, layer, grid_rank):
    tail = (0,) * (len(shape) - 2)
    if grid_rank == 2:
        return pl.BlockSpec((None, 1) + shape[2:], lambda b, s: (layer, b) + tail)
    return pl.BlockSpec((None, 1) + shape[2:], lambda b: (layer, b) + tail)


_ANY_SPEC = pl.BlockSpec(memory_space=pl.ANY)


def _even_mix(pre, sg_in, sh_in, sg_acc, sh_acc, layer, gn, hn, bsz, t_len):
    if t_len % CHUNK == 0:
        c, tb = CHUNK, CHUNK * CHUNKS_PER_STEP
        assert t_len % tb == 0
    else:
        c, tb = SEQ_ROWS, t_len
        assert t_len < c
    pre = [a.reshape(bsz, t_len, a.shape[-1]) for a in pre]
    msum = jnp.asarray(_decay_sum_matrix(c), BF16)
    lev = jnp.asarray(_level_index(c))
    ones_cv = jnp.ones((c, PAIR_V), BF16)
    row = lambda w: pl.BlockSpec((1, tb, w), lambda b, s: (b, s, 0))
    st = lambda a: _layer_state_spec(a.shape, layer, 2)
    scratch = [pltpu.VMEM((H_A // 2, 2 * DK_A, PAIR_V), F32),
               pltpu.VMEM((H_B // 2, 2 * N_B, PAIR_V), F32)]
    if tb < c:
        scratch += [pltpu.VMEM((c, w), F32) for w in EVEN_WIDTHS]
    o, sg, sh = pl.pallas_call(
        functools.partial(_even_mix_kernel, c=c),
        out_shape=[jax.ShapeDtypeStruct((bsz, t_len, D_MODEL), F32),
                   jax.ShapeDtypeStruct(sg_acc.shape, F32),
                   jax.ShapeDtypeStruct(sh_acc.shape, F32)],
        grid=(bsz, t_len // tb),
        in_specs=[_ANY_SPEC, _ANY_SPEC] + [row(w) for w in EVEN_WIDTHS] + [
            _const_spec(msum.shape), _const_spec(lev.shape), _const_spec(ones_cv.shape),
            st(sg_in), st(sh_in), _pspec(gn), _pspec(hn)],
        out_specs=[row(D_MODEL), st(sg_acc), st(sh_acc)],
        scratch_shapes=scratch,
        input_output_aliases={0: 1, 1: 2},
        compiler_params=_cparams(("parallel", "arbitrary")),
        name="even_mix",
    )(sg_acc, sh_acc, *pre, msum, lev, ones_cv, sg_in, sh_in, _parg(gn), _parg(hn))
    return o.reshape(bsz * t_len, D_MODEL), sg, sh


def _rwkv_pre_kernel(*refs, t_len, has_vmix):
    (h_ref, xl_ref, gm_ref, mu_ref, wr_ref, wk_ref, wv_ref, w0_ref, w1_ref, w2_ref,
     a0_ref, a1_ref, a2_ref, g1_ref, g2_ref, kk_ref, ka_ref) = refs[:17]
    pos = 17
    if has_vmix:
        vf_ref, v0_ref, v1_ref, v2_ref = refs[pos:pos + 4]
        pos += 4
    (r_out, lw_out, k_out, v_out, kkr_out, a_out, g_out, xn_out, carry) = refs[pos:]

    tm = h_ref.shape[0]
    xn = _rms(h_ref[...], gm_ref[...])
    xn_out[...] = xn
    rows = _iota2((tm, 1), 0)
    t_idx = (pl.program_id(0) * tm + rows) & (t_len - 1)
    prev = pltpu.roll(xn, 1, 0)
    prev = jnp.where(rows == 0, carry[0:1, :], prev)
    prev = jnp.where(t_idx == 0, xl_ref[...], prev)
    carry[0:1, :] = xn[tm - 1:tm, :]
    dx = prev - xn

    def mixed(i):
        return (xn + dx * mu_ref[i:i + 1, :]).astype(BF16)

    _put(r_out, _dot(mixed(0), wr_ref[...]))
    dec = w0_ref[...] + _mm(jnp.tanh(_dot(mixed(1), w1_ref[...])), w2_ref[...])
    _put(lw_out, -_sigmoid(dec) * math.exp(-0.5))
    k = _dot(mixed(2), wk_ref[...])
    xv = mixed(3)
    v = _dot(xv, wv_ref[...])
    a = _sigmoid(a0_ref[...] + _mm(_dot(mixed(4), a1_ref[...]), a2_ref[...]))
    _put(a_out, a)
    _put(g_out, _mm(_sigmoid(_dot(mixed(5), g1_ref[...])), g2_ref[...]))
    _put(kkr_out, k * kk_ref[...])
    _put(k_out, k * (1.0 + (a - 1.0) * ka_ref[...]))
    if has_vmix:
        gate = _sigmoid(v0_ref[...] + _mm(_dot(xv, v1_ref[...]), v2_ref[...]))
        v = v + (vf_ref[...].astype(F32) - v) * gate
    _put(v_out, v)


def _rwkv_pre(h, xl, t_len, wts, vmix, tm):
    m = h.shape[0]
    assert t_len & (t_len - 1) == 0
    row = pl.BlockSpec((tm, D_MODEL), lambda i: (i, 0))
    args = [h, xl] + [_parg(w) for w in wts]
    specs = [row, row] + [_pspec(w) for w in wts]
    if vmix is not None:
        args += [vmix[0]] + [_parg(w) for w in vmix[1:]]
        specs += [row] + [_pspec(w) for w in vmix[1:]]
    return pl.pallas_call(
        functools.partial(_rwkv_pre_kernel, t_len=t_len, has_vmix=vmix is not None),
        out_shape=[jax.ShapeDtypeStruct((m, D_MODEL), d) for d in RWKV_DTYPES],
        grid=(m // tm,),
        in_specs=specs,
        out_specs=[row] * 8,
        scratch_shapes=[pltpu.VMEM((8, D_MODEL), F32)],
        compiler_params=_cparams(("arbitrary",)),
        name="rwkv_pre",
    )(*args)


def _mm_pair(x, y, first, x_terms=2, y_terms=2):
    m = x.shape[0]
    xh = x.astype(BF16)
    yh = y.astype(BF16)
    ybh = _stack_heads(yh, first)
    if x_terms == 2:
        both = _dot(jnp.concatenate([xh, (x - xh.astype(F32)).astype(BF16)], axis=0), ybh)
        acc = both[0:m] + both[m:2 * m]
    else:
        acc = _dot(xh, ybh)
    if y_terms == 2:
        acc = acc + _dot(xh, _stack_heads((y - yh.astype(F32)).astype(BF16), first))
    return acc


def _segsum(x, seg):
    c = x.shape[0]
    n = D_MODEL // SEG
    xs = jnp.concatenate([x[:, i * SEG:(i + 1) * SEG] for i in range(n)], axis=0)
    ys = _sum01_right(xs, seg)
    return jnp.concatenate([ys[i * c:(i + 1) * c] for i in range(n)], axis=1)


def _rwkv_head_out(o, r, k, v, g, rk_ref, lnw_ref, lnb_ref, seg):
    mean = _segsum(o, seg) * (1.0 / N_C)
    var = _segsum(jnp.square(o - mean), seg) * (1.0 / N_C)
    on = (o - mean) * lax.rsqrt(var + GN_EPS) * lnw_ref[...] + lnb_ref[...]
    bonus = _segsum(r * k * rk_ref[...], seg) * v
    return (on + bonus) * g


def _rwkv_pair(a_t, r_t, b_t, k_t, b_e, k_e, v, w_tot, y_ref, out, c):
    t_i = _iota2((c, LANE), 0)
    s_i = _iota2((c, LANE), 1) & (c - 1)
    first = _iota2((1, LANE), 1) < N_C
    strict = s_i < t_i
    incl = s_i <= t_i
    eye = (s_i == t_i).astype(F32)
    same_blk = (s_i // SUB) == (t_i // SUB)

    lhs = jnp.concatenate([a_t, r_t], axis=0).astype(BF16)
    rhs = jnp.concatenate([_stack_heads(b_t.astype(BF16), first),
                           _stack_heads(k_t.astype(BF16), first)], axis=0)
    m1 = _dot_nt(lhs, rhs)
    l_ab = jnp.where(strict, m1[0:c, 0:LANE], 0.0).astype(BF16).astype(F32)
    l_ak = jnp.where(strict, m1[0:c, LANE:2 * LANE], 0.0)
    m_rb = jnp.where(incl, m1[c:2 * c, 0:LANE], 0.0)
    m_rk = jnp.where(incl, m1[c:2 * c, LANE:2 * LANE], 0.0)
    y0 = y_ref[...]
    xs = _dot_nt(lhs, y0.astype(BF16))
    v_bd = _stack_heads(v.astype(BF16), first)
    rhs0 = xs[0:c] + _dot(l_ak.astype(BF16), v_bd)
    yield
    l_bd = jnp.where(same_blk, l_ab, 0.0)
    l_off = l_ab - l_bd
    l2 = _mm_pair(l_bd, l_bd, first, 1, 1)
    p = eye - l_bd
    yield
    both = _mm_pair(jnp.concatenate([p, l2], axis=0), l2, first)
    p = p + both[0:c]
    l4 = both[c:2 * c]
    yield
    both = _mm_pair(jnp.concatenate([p, l4], axis=0), l4, first)
    p = p + both[0:c]
    l8 = both[c:2 * c]
    yield
    p = p + _mm_pair(p, l8, first)
    yield
    n = _mm_pair(p, l_off, first, 2, 1)
    yield
    n2 = _mm_pair(n, n, first)
    q = eye - n
    yield
    q = q + _mm_pair(q, n2, first)
    yield
    w = _mm_pair(p, rhs0, first, 2, 1)
    yield
    u = -_mm_pair(q, w, first, 2, 1)
    yield
    mr = jnp.concatenate([m_rb, m_rk], axis=1).astype(BF16)
    uv_bd = jnp.concatenate([_stack_heads(u.astype(BF16), first), v_bd], axis=0)
    out.append(xs[c:2 * c] + _dot(mr, uv_bd))
    uv = jnp.concatenate([u, v], axis=0).astype(BF16)
    bk = jnp.concatenate([b_e, k_e], axis=0).astype(BF16)
    same_head = (_iota2((LANE, 1), 0) // N_C) == (_iota2((1, LANE), 1) // N_C)
    y_ref[...] = y0 * jnp.exp(w_tot) + jnp.where(same_head, _dot_tn(uv, bk), 0.0)


def _rwkv_mix_kernel(s_acc_ref, r_ref, lw_ref, k_ref, v_ref, kkr_ref, a_ref, g_ref, s0_ref, rk_ref,
                     lnw_ref, lnb_ref, tri_ref, seg_ref, o_ref, s_out_ref, y_scr, *, c):
    del s_acc_ref
    step = pl.program_id(1)
    n_pair = H_C // 2

    @pl.when(step == 0)
    def _():
        y_scr[...] = jnp.zeros(y_scr.shape, F32)
        for p in range(n_pair):
            y_scr[p, 0:N_C, 0:N_C] = s0_ref[0, 2 * p]
            y_scr[p, N_C:LANE, N_C:LANE] = s0_ref[0, 2 * p + 1]

    ins = (r_ref, lw_ref, k_ref, v_ref, kkr_ref, a_ref, g_ref)
    seg = seg_ref[...]
    for r0 in range(0, o_ref.shape[1], c):
        r, lw, k, v, kk, a, g = [x[0, r0:r0 + c, :].astype(F32) for x in ins]
        kk = kk * lax.rsqrt(jnp.maximum(_segsum(kk * kk, seg), KK_EPS))
        cw = _sum01_left(tri_ref[...], lw)
        w_tot = cw[c - 1:c]
        kka = kk * a
        a_t = kk * jnp.exp(cw - lw)
        r_t = r * jnp.exp(cw)
        e_inv = jnp.exp(-cw)
        b_t = kka * e_inv
        k_t = k * e_inv
        e_end = jnp.exp(w_tot - cw)
        b_e = kka * e_end
        k_e = k * e_end

        outs = [[] for _ in range(n_pair)]
        gens = []
        for p in range(n_pair):
            sl = slice(p * LANE, (p + 1) * LANE)
            gens.append(_rwkv_pair(a_t[:, sl], r_t[:, sl], b_t[:, sl], k_t[:, sl], b_e[:, sl],
                                   k_e[:, sl], v[:, sl], w_tot[:, sl], y_scr.at[p], outs[p], c))
        _lockstep(gens)
        o = jnp.concatenate([outs[p][0] for p in range(n_pair)], axis=1)
        o_ref[0, r0:r0 + c, :] = _rwkv_head_out(o, r, k, v, g, rk_ref, lnw_ref, lnb_ref, seg)

    @pl.when(step == pl.num_programs(1) - 1)
    def _():
        for p in range(n_pair):
            s_out_ref[0, 2 * p] = y_scr[p, 0:N_C, 0:N_C]
            s_out_ref[0, 2 * p + 1] = y_scr[p, N_C:LANE, N_C:LANE]


def _rwkv_mix(pre, s_in, s_acc, layer, rk, lnw, lnb, bsz, t_len):
    c = CHUNK
    tb = c * CHUNKS_PER_STEP
    assert 2 * c == LANE and c // SUB == 4 and t_len % tb == 0
    pre = [a.reshape(bsz, t_len, D_MODEL) for a in pre]
    tri = jnp.asarray(np.tril(np.ones((c, c), np.float32)), BF16)
    seg = jnp.asarray(np.kron(np.eye(SEG // N_C, dtype=np.float32),
                              np.ones((N_C, N_C), np.float32)), BF16)
    row = pl.BlockSpec((1, tb, D_MODEL), lambda b, s: (b, s, 0))
    st = _layer_state_spec(s_in.shape, layer, 2)
    scratch = [pltpu.VMEM((H_C // 2, LANE, LANE), F32)]
    o, s_new = pl.pallas_call(
        functools.partial(_rwkv_mix_kernel, c=c),
        out_shape=[jax.ShapeDtypeStruct((bsz, t_len, D_MODEL), F32),
                   jax.ShapeDtypeStruct(s_acc.shape, F32)],
        grid=(bsz, t_len // tb),
        in_specs=[_ANY_SPEC] + [row] * 7 + [st, _pspec(rk), _pspec(lnw), _pspec(lnb),
                                            _const_spec(tri.shape), _const_spec(seg.shape)],
        out_specs=[row, st],
        scratch_shapes=scratch,
        input_output_aliases={0: 1},
        compiler_params=_cparams(("parallel", "arbitrary")),
        name="rwkv_mix",
    )(s_acc, *pre, s_in, _parg(rk), _parg(lnw), _parg(lnb), tri, seg)
    return o.reshape(bsz * t_len, D_MODEL), s_new


def _rwkv_lane_kernel(s_acc_ref, r_ref, lw_ref, k_ref, v_ref, kkr_ref, a_ref, g_ref, s0_ref, rk_ref,
                      lnw_ref, lnb_ref, o_ref, s_out_ref, v_scr, o_scr):
    del s_acc_ref
    t_len = r_ref.shape[0]
    for t in range(t_len):
        kk = kkr_ref[t].astype(F32)
        kk = kk * lax.rsqrt(jnp.maximum(jnp.sum(kk * kk, axis=0, keepdims=True), KK_EPS))
        kka = kk * a_ref[t].astype(F32)
        w = jnp.exp(lw_ref[t])
        k = k_ref[t].astype(F32)
        r = r_ref[t].astype(F32)
        v = v_ref[t].astype(F32)
        v_scr[...] = v
        src = s0_ref if t == 0 else s_out_ref

        def value_row(i, carry, src=src, kk=kk, kka=kka, w=w, k=k, r=r):
            s_i = src[i]
            s_kk = jnp.sum(s_i * kk, axis=0, keepdims=True)
            s_i = s_i * w - s_kk * kka + v_scr[pl.ds(i, 1), :] * k
            s_out_ref[i] = s_i
            o_scr[pl.ds(i, 1), :] = jnp.sum(s_i * r, axis=0, keepdims=True)
            return carry

        lax.fori_loop(0, N_C, value_row, 0, unroll=4)
        o = o_scr[...]
        mean = jnp.mean(o, axis=0, keepdims=True)
        var = jnp.mean(jnp.square(o - mean), axis=0, keepdims=True)
        on = (o - mean) * lax.rsqrt(var + GN_EPS) * lnw_ref[...] + lnb_ref[...]
        bonus = jnp.sum(r * k * rk_ref[...], axis=0, keepdims=True) * v
        o_ref[t] = (on + bonus) * g_ref[t].astype(F32)


def _rwkv_lane(pre, s_in, s_acc, layer, rk, lnw, lnb, bsz, t_len):
    assert bsz % LANE == 0 and t_len < CHUNK
    to_lanes = lambda a: a.reshape(bsz, t_len, D_MODEL).transpose(1, 2, 0)
    pre = [to_lanes(a) for a in pre]
    chan = lambda p: jnp.broadcast_to(p[0][p[1]].reshape(D_MODEL, 1), (D_MODEL, bsz))
    tok = pl.BlockSpec((t_len, N_C, bsz), lambda h: (0, h, 0))
    par = pl.BlockSpec((N_C, bsz), lambda h: (h, 0))
    st = pl.BlockSpec((None, None) + s_in.shape[2:], lambda h: (layer, h, 0, 0, 0))
    o, s_new = pl.pallas_call(
        _rwkv_lane_kernel,
        out_shape=[jax.ShapeDtypeStruct((t_len, D_MODEL, bsz), F32),
                   jax.ShapeDtypeStruct(s_acc.shape, F32)],
        grid=(H_C,),
        in_specs=[_ANY_SPEC] + [tok] * 7 + [st, par, par, par],
        out_specs=[tok, st],
        scratch_shapes=[pltpu.VMEM((N_C, bsz), F32), pltpu.VMEM((N_C, bsz), F32)],
        input_output_aliases={0: 1},
        compiler_params=_cparams(("parallel",)),
        name="rwkv_lane",
    )(s_acc, *pre, s_in, chan(rk), chan(lnw), chan(lnb))
    return o.transpose(2, 0, 1).reshape(bsz * t_len, D_MODEL), s_new


def _post_kernel(h_ref, o_ref, wo_ref, gf_ref, wg_ref, wu_ref, wd_ref, gp_ref, wpg_ref,
                 p_ref, wple_ref, gfin_ref, out_ref, *, final):
    h1 = h_ref[...] + _dot(o_ref[...].astype(BF16), wo_ref[...])
    x2 = _rms(h1, gf_ref[...]).astype(BF16)
    h2 = h1
    for lo in range(0, FFN_DIM, FFN_SPLIT):
        gate = _dot(x2, wg_ref[:, lo:lo + FFN_SPLIT])
        up = _dot(x2, wu_ref[:, lo:lo + FFN_SPLIT])
        act = (_silu(gate) * up).astype(BF16)
        h2 = h2 + _dot(act, wd_ref[lo:lo + FFN_SPLIT, :])
    x3 = _rms(h2, gp_ref[...]).astype(BF16)
    gate = _sigmoid(_dot(x3, wpg_ref[...]))
    emb = _dot(p_ref[...].astype(BF16), wple_ref[...])
    h3 = h2 + gate * emb
    out_ref[...] = _rms(h3, gfin_ref[...]) if final else h3


def _post(h, o, p, layer, wts, final, tm):
    m = h.shape[0]
    wo, gf, wg, wu, wd, gp, wpg, wple, gfin = wts
    row = lambda w: pl.BlockSpec((tm, w), lambda i: (i, 0))
    p_spec = pl.BlockSpec((None, tm, PLE_DIM), lambda i: (layer, i, 0))
    return pl.pallas_call(
        functools.partial(_post_kernel, final=final),
        out_shape=jax.ShapeDtypeStruct((m, D_MODEL), F32),
        grid=(m // tm,),
        in_specs=[row(D_MODEL), row(D_MODEL), _pspec(wo), _pspec(gf), _pspec(wg), _pspec(wu),
                  _pspec(wd), _pspec(gp), _pspec(wpg), p_spec, _pspec(wple), _pspec(gfin)],
        out_specs=row(D_MODEL),
        compiler_params=_cparams(("parallel",)),
        name="post",
    )(h, o, *[_parg(w) for w in (wo, gf, wg, wu, wd, gp, wpg)], p, _parg(wple), _parg(gfin))


def _run_group(x, p, s_gla, s_hgrn, s_rwkv, s_shift, wt):
    bsz, t_len, _ = x.shape
    m = bsz * t_len
    tm = min(512, m)
    h = x.reshape(m, D_MODEL)
    p = p.reshape(DEPTH, m, PLE_DIM)
    long_seq = t_len % CHUNK == 0
    if not long_seq:
        s_rwkv = s_rwkv.transpose(0, 2, 3, 4, 1)
    acc_gla = jnp.zeros(s_gla.shape, F32)
    acc_hgrn = jnp.zeros(s_hgrn.shape, F32)
    acc_rwkv = jnp.zeros(s_rwkv.shape, F32)
    out_shift = []
    v_first = None
    par = lambda name, layer: (wt[name], layer)
    for i in range(DEPTH):
        j = i // 2
        gm = par('norm_mix', i)
        if i % 2 == 0:
            pre = _even_pre(h, [gm, par('w_cat', j), par('w_gk1', j), par('w_gk2', j),
                                par('b_gk', j), wt['hgrn_gamma']], j, tm)
            o, acc_gla, acc_hgrn = _even_mix(pre, s_gla, s_hgrn, acc_gla, acc_hgrn, j,
                                             par('gla_norm', j), par('hgrn_norm', j), bsz, t_len)
            w_o = par('w_out_even', j)
        else:
            xl = jnp.broadcast_to(s_shift[j][:, None, :], (bsz, t_len, D_MODEL)).reshape(m, D_MODEL)
            wts = [gm] + [par(n, j) for n in ('rw_mu', 'rw_wr', 'rw_wk', 'rw_wv', 'rw_w0', 'rw_w1',
                                               'rw_w2', 'rw_a0', 'rw_a1', 'rw_a2', 'rw_g1', 'rw_g2',
                                               'rw_kk', 'rw_ka')]
            vmix = None
            if j > 0:
                vmix = [v_first] + [par(n, j - 1) for n in ('rw_v0', 'rw_v1', 'rw_v2')]
            r, lw, k, v, kkr, a, g, xn = _rwkv_pre(h, xl, t_len, wts, vmix, tm)
            if j == 0:
                v_first = v
            rwkv = _rwkv_mix if long_seq else _rwkv_lane
            o, acc_rwkv = rwkv((r, lw, k, v, kkr, a, g), s_rwkv, acc_rwkv, j, par('rw_rk', j),
                               par('rw_lnw', j), par('rw_lnb', j), bsz, t_len)
            out_shift.append(xn.reshape(bsz, t_len, D_MODEL)[:, -1])
            w_o = par('rw_wo', j)
        post_w = [w_o] + [par(n, i) for n in ('norm_ffn', 'w_ffn_gate', 'w_ffn_up', 'w_ffn_down',
                                              'norm_ple', 'w_ple_gate', 'w_ple')] + [wt['norm_final']]
        h = _post(h, o, p, i, post_w, i == DEPTH - 1, tm)
    if not long_seq:
        acc_rwkv = acc_rwkv.transpose(0, 4, 1, 2, 3)
    return (h.reshape(bsz, t_len, D_MODEL), acc_gla, acc_hgrn, acc_rwkv, jnp.stack(out_shift))


def kernel(x_prompt, x_sample, p_prompt, p_sample, state_gla, state_hgrn, state_rwkv, state_shift, norm_mix, norm_ffn, norm_ple, norm_final, w_in_even, w_gk2, b_gk, gla_norm, hgrn_gamma, hgrn_norm, w_out_even, rw_mu, rw_wr, rw_wk, rw_wv, rw_wo, rw_w0, rw_w1, rw_w2, rw_a0, rw_a1, rw_a2, rw_v0, rw_v1, rw_v2, rw_g1, rw_g2, rw_kk, rw_ka, rw_rk, rw_lnw, rw_lnb, w_ffn_gate, w_ffn_up, w_ffn_down, w_ple, w_ple_gate):
    bf = lambda w: w.astype(BF16)
    w_cat = jnp.concatenate([w_in_even[:, :, 0:1536], w_in_even[:, :, 1552:3600]], axis=-1)
    w_gk1 = jnp.pad(w_in_even[:, :, 1536:1552], ((0, 0), (0, 0), (0, LANE - GK_RANK)))
    w_gk2p = jnp.pad(w_gk2, ((0, 0), (0, LANE - GK_RANK), (0, 0)))
    pad_c = lambda w: jnp.pad(w, ((0, 0), (0, 0), (0, LANE - w.shape[2])))
    pad_r = lambda w: jnp.pad(w, ((0, 0), (0, LANE - w.shape[1]), (0, 0)))
    vec = lambda w: w.reshape(w.shape[0], 1, -1)
    wt = dict(
        norm_mix=vec(norm_mix), norm_ffn=vec(norm_ffn), norm_ple=vec(norm_ple),
        norm_final=norm_final.reshape(1, -1),
        w_cat=bf(w_cat), w_gk1=bf(w_gk1), w_gk2=bf(w_gk2p), b_gk=vec(b_gk),
        gla_norm=vec(gla_norm), hgrn_gamma=hgrn_gamma, hgrn_norm=vec(hgrn_norm),
        w_out_even=bf(w_out_even),
        rw_mu=rw_mu, rw_wr=bf(rw_wr), rw_wk=bf(rw_wk), rw_wv=bf(rw_wv), rw_wo=bf(rw_wo),
        rw_w0=vec(rw_w0), rw_w1=bf(pad_c(rw_w1)), rw_w2=bf(pad_r(rw_w2)),
        rw_a0=vec(rw_a0), rw_a1=bf(pad_c(rw_a1)), rw_a2=bf(pad_r(rw_a2)),
        rw_v0=vec(rw_v0), rw_v1=bf(pad_c(rw_v1)), rw_v2=bf(pad_r(rw_v2)),
        rw_g1=bf(rw_g1), rw_g2=bf(rw_g2), rw_kk=vec(rw_kk), rw_ka=vec(rw_ka), rw_rk=vec(rw_rk),
        rw_lnw=vec(rw_lnw), rw_lnb=vec(rw_lnb),
        w_ffn_gate=bf(w_ffn_gate), w_ffn_up=bf(w_ffn_up), w_ffn_down=bf(w_ffn_down),
        w_ple=bf(w_ple), w_ple_gate=bf(w_ple_gate),
    )
    bp = x_prompt.shape[0]
    zeros = lambda s: jnp.zeros((s.shape[0], bp) + s.shape[2:], F32)
    y_p, gla_p, hgrn_p, rwkv_p, shift_p = _run_group(
        x_prompt, p_prompt, zeros(state_gla), zeros(state_hgrn), zeros(state_rwkv),
        zeros(state_shift), wt)
    y_s, gla_s, hgrn_s, rwkv_s, shift_s = _run_group(
        x_sample, p_sample, state_gla, state_hgrn, state_rwkv, state_shift, wt)
    return (y_p, y_s, gla_p, hgrn_p, rwkv_p, shift_p, gla_s, hgrn_s, rwkv_s, shift_s)
```

```python
import functools
import math

import numpy as np
import jax
import jax.numpy as jnp
from jax import lax
from jax.experimental import pallas as pl
from jax.experimental.pallas import tpu as pltpu

F32 = jnp.float32
BF16 = jnp.bfloat16

D_MODEL = 1024
DEPTH = 4
MIX_A = 512
H_A = 4
DV_A = 128
DK_A = 64
GK_RANK = 16
GK_NORM = 16.0
H_B = 4
DV_B = 128
N_B = 128
LB_FLOOR = 1e-20
N_C = 64
H_C = D_MODEL // N_C
FFN_DIM = 2816
PLE_DIM = 256
RMS_EPS = 1e-6
GN_EPS = 64e-5
KK_EPS = 1e-24

LANE = 128
CHUNK = 64
CHUNKS_PER_STEP = 4
SUB = 16
VMEM_LIMIT = 56 * 1024 * 1024
FFN_SPLIT = 1408
PAIR_V = 2 * DV_A
SEG = 256
SEQ_ROWS = 16

EVEN_WIDTHS = (256, 256, 512, 512, 256, 512, 512, 512, 512, 512)
EVEN_DTYPES = (BF16, BF16, BF16, BF16, F32, BF16, BF16, BF16, BF16, F32)
RWKV_DTYPES = (BF16, F32, BF16, BF16, BF16, BF16, BF16)


def _cparams(sem):
    return pltpu.CompilerParams(dimension_semantics=sem, vmem_limit_bytes=VMEM_LIMIT)


def _const_spec(shape):
    nd = len(shape)
    return pl.BlockSpec(shape, lambda *_: (0,) * nd, pipeline_mode=pl.Buffered(1))


def _pspec(p):
    if not isinstance(p, tuple):
        return _const_spec(p.shape)
    a, layer = p
    nd = a.ndim - 1
    return pl.BlockSpec((None,) + a.shape[1:], lambda *_: (layer,) + (0,) * nd,
                        pipeline_mode=pl.Buffered(1))


def _parg(p):
    return p[0] if isinstance(p, tuple) else p


def _rms(x, g, eps=RMS_EPS):
    return x * lax.rsqrt(jnp.mean(x * x, axis=-1, keepdims=True) + eps) * g


def _sigmoid(x):
    return 1.0 / (1.0 + jnp.exp(-x))


def _silu(x):
    return x * _sigmoid(x)


def _log_sigmoid(x):
    return jnp.minimum(x, 0.0) - jnp.log1p(jnp.exp(-jnp.abs(x)))


def _dot(a, b):
    return jnp.dot(a, b, preferred_element_type=F32)


def _dot_nt(a, b):
    return lax.dot_general(a, b, (((1,), (1,)), ((), ())), preferred_element_type=F32)


def _dot_tn(a, b):
    return lax.dot_general(a, b, (((0,), (0,)), ((), ())), preferred_element_type=F32)


def _mm(a, b):
    return _dot(a.astype(BF16), b.astype(BF16))


def _split3(x):
    hi = x.astype(BF16)
    r1 = x - hi.astype(F32)
    mid = r1.astype(BF16)
    return hi, mid, (r1 - mid.astype(F32)).astype(BF16)


def _sum01_left(m01, x):
    hi, mid, lo = _split3(x)
    return _dot(m01, hi) + _dot(m01, mid) + _dot(m01, lo)


def _sum01_right(x, m01):
    n = x.shape[0]
    y = _dot(jnp.concatenate(_split3(x), axis=0), m01)
    return y[0:n] + y[n:2 * n] + y[2 * n:3 * n]


def _iota2(shape, axis):
    return lax.broadcasted_iota(jnp.int32, shape, axis)


def _stack_heads(x, first):
    return jnp.concatenate([jnp.where(first, x, 0), jnp.where(first, 0, x)], axis=0)


def _lockstep(gens):
    gens = list(gens)
    while gens:
        alive = []
        for g in gens:
            try:
                next(g)
                alive.append(g)
            except StopIteration:
                pass
        gens = alive


def _load_rows(ref, scr, c):
    t = ref.shape[1]
    if t == c:
        return ref[0]
    scr[...] = jnp.zeros(scr.shape, F32)
    scr[0:t, :] = ref[0].astype(F32)
    return scr[...]


def _put(ref, val):
    ref[...] = val.astype(ref.dtype)


def _even_pre_kernel(h_ref, gm_ref, w_ref, wgk1_ref, wgk2_ref, bgk_ref, gam_ref,
                     qa_ref, ka_ref, va_ref, ga_ref, lga_ref,
                     qb_ref, kb_ref, vb_ref, gb_ref, lgb_ref, *, layer):
    xn = _rms(h_ref[...], gm_ref[...]).astype(BF16)

    def proj(lo, hi):
        return _dot(xn, w_ref[:, lo:hi])

    _put(qa_ref, proj(0, 256) * DK_A ** -0.5)
    _put(ka_ref, proj(256, 512))
    _put(va_ref, proj(512, 1024))
    _put(ga_ref, _silu(proj(1024, 1536)))
    gk_lr = _dot(xn, wgk1_ref[...])
    gk = _mm(gk_lr, wgk2_ref[...]) + bgk_ref[...]
    _put(lga_ref, _log_sigmoid(gk) / GK_NORM)

    gam = gam_ref[...]
    n_even = gam.shape[0]
    gmax = gam[0:1]
    for i in range(1, n_even):
        gmax = jnp.maximum(gmax, gam[i:i + 1])
    es = [jnp.exp(gam[i:i + 1] - gmax) for i in range(n_even)]
    den = es[0]
    for i in range(1, n_even):
        den = den + es[i]
    sm = [e / den for e in es]
    cum = sm[0]
    for i in range(1, layer + 1):
        cum = cum + sm[i]
    lb = cum - sm[0]

    _put(qb_ref, _silu(proj(1536, 2048)) * N_B ** -0.5)
    z = proj(2048, 2560)
    _put(lgb_ref, jnp.log(jnp.maximum(lb, LB_FLOOR) + (1.0 - lb) * _sigmoid(z)))
    _put(kb_ref, (1.0 - lb) * _sigmoid(-z))
    _put(vb_ref, proj(2560, 3072))
    _put(gb_ref, _silu(proj(3072, 3584)))


def _even_pre(h, params, layer, tm):
    m = h.shape[0]
    row = lambda w: pl.BlockSpec((tm, w), lambda i: (i, 0))
    return pl.pallas_call(
        functools.partial(_even_pre_kernel, layer=layer),
        out_shape=[jax.ShapeDtypeStruct((m, w), d) for w, d in zip(EVEN_WIDTHS, EVEN_DTYPES)],
        grid=(m // tm,),
        in_specs=[row(D_MODEL)] + [_pspec(p) for p in params],
        out_specs=[row(w) for w in EVEN_WIDTHS],
        compiler_params=_cparams(("parallel",)),
        name="even_pre",
    )(h, *[_parg(p) for p in params])


def _level_widths(c):
    return [c >> i for i in range(1, int(math.log2(c)) + 1)]


def _decay_sum_matrix(c):
    i = np.arange(c)[:, None]
    t = np.arange(c)[None, :]
    blocks = [t <= i, t > i]
    for w in _level_widths(c):
        ref = (i // (2 * w)) * (2 * w) + w
        blocks.append(np.where(i >= ref, (t > ref) & (t <= i), (t > i) & (t <= ref)))
    return np.concatenate(blocks, axis=0).astype(np.float32)


def _level_index(c):
    t = np.arange(c)[:, None]
    s = np.arange(c)[None, :]
    lev = np.full((c, c), -1, np.int32)
    lev[t == s] = 0
    for li, w in enumerate(_level_widths(c)):
        m = (t // (2 * w) == s // (2 * w)) & (t % (2 * w) >= w) & (s % (2 * w) < w)
        lev[m] = li + 1
    return np.tile(lev, (1, 2))


def _gated_pair(q, k, v, lg, e_all, lev, ones_cv, y_ref, out, c):
    kw = q.shape[1]
    nlev = len(_level_widths(c))
    first_k = _iota2((1, kw), 1) < kw // 2
    first_v = _iota2((1, PAIR_V), 1) < PAIR_V // 2
    parts = []
    for l in range(nlev + 1):
        if l == 0:
            q_l, k_l = q, k
        else:
            e = e_all[(l + 1) * c:(l + 2) * c]
            q_l, k_l = q * e, k * e
        parts.append(_dot_nt(q_l.astype(BF16), _stack_heads(k_l.astype(BF16), first_k)))
    y0 = y_ref[...]
    o_inter = _mm(q * e_all[0:c], y0)
    upd = _dot_tn((k * e_all[c:2 * c]).astype(BF16), v.astype(BF16))
    l_hi, l_mid, l_lo = _split3(lg)
    b_last = _dot_tn(l_hi, ones_cv) + _dot_tn(l_mid, ones_cv) + _dot_tn(l_lo, ones_cv)
    yield
    a_mat = jnp.where(lev == 0, parts[0], 0.0)
    for l in range(1, nlev + 1):
        a_mat = jnp.where(lev == l, parts[l], a_mat)
    o = o_inter + _dot(a_mat.astype(BF16), _stack_heads(v.astype(BF16), first_v))
    same_head = (_iota2((kw, 1), 0) // (kw // 2)) == (_iota2((1, PAIR_V), 1) // (PAIR_V // 2))
    y_ref[...] = jnp.exp(b_last) * y0 + jnp.where(same_head, upd, 0.0)
    out.append(o)


def _even_mix_kernel(sg_acc_ref, sh_acc_ref, qa_ref, ka_ref, va_ref, ga_ref, lga_ref, qb_ref,
                     kb_ref, vb_ref, gb_ref, lgb_ref, msum_ref, lev_ref, ones_ref, sg0_ref,
                     sh0_ref, gn_ref, hn_ref, o_ref, sg_ref, sh_ref, ya_scr, yb_scr, *pads, c):
    del sg_acc_ref, sh_acc_ref
    step = pl.program_id(1)
    ka_w = 2 * DK_A
    kb_w = 2 * N_B

    @pl.when(step == 0)
    def _():
        ya_scr[...] = jnp.zeros(ya_scr.shape, F32)
        yb_scr[...] = jnp.zeros(yb_scr.shape, F32)
        for p in range(H_A // 2):
            ya_scr[p, 0:DK_A, 0:DV_A] = sg0_ref[0, 2 * p]
            ya_scr[p, DK_A:ka_w, DV_A:PAIR_V] = sg0_ref[0, 2 * p + 1]
        for p in range(H_B // 2):
            yb_scr[p, 0:N_B, 0:DV_B] = sh0_ref[0, 2 * p]
            yb_scr[p, N_B:kb_w, DV_B:PAIR_V] = sh0_ref[0, 2 * p + 1]

    ins = (qa_ref, ka_ref, va_ref, ga_ref, lga_ref, qb_ref, kb_ref, vb_ref, gb_ref, lgb_ref)
    msum = msum_ref[...]
    lev = lev_ref[...]
    ones_cv = ones_ref[...]
    rows = o_ref.shape[1]
    for r0 in range(0, rows, c):
        if pads:
            vals = [_load_rows(r, s, c) for r, s in zip(ins, pads)]
        else:
            vals = [r[0, r0:r0 + c, :].astype(F32) for r in ins]
        qa, ka, va, ga, lga, qb, kb, vb, gb, lgb = vals
        t_out = min(c, rows)
        ea = jnp.exp(_sum01_left(msum, lga))
        eb = jnp.exp(_sum01_left(msum, lgb))

        outs_a = [[] for _ in range(H_A // 2)]
        outs_b = [[] for _ in range(H_B // 2)]
        gens = []
        for p in range(H_A // 2):
            ks = slice(p * ka_w, (p + 1) * ka_w)
            vs = slice(p * PAIR_V, (p + 1) * PAIR_V)
            gens.append(_gated_pair(qa[:, ks], ka[:, ks], va[:, vs], lga[:, ks], ea[:, ks], lev,
                                    ones_cv, ya_scr.at[p], outs_a[p], c))
        for p in range(H_B // 2):
            ks = slice(p * kb_w, (p + 1) * kb_w)
            vs = slice(p * PAIR_V, (p + 1) * PAIR_V)
            gens.append(_gated_pair(qb[:, ks], kb[:, ks], vb[:, vs], lgb[:, ks], eb[:, ks], lev,
                                    ones_cv, yb_scr.at[p], outs_b[p], c))
        _lockstep(gens)

        for h in range(H_A):
            o = outs_a[h // 2][0][:, (h % 2) * DV_A:(h % 2 + 1) * DV_A]
            vs = slice(h * DV_A, (h + 1) * DV_A)
            res = _rms(o, gn_ref[...]) * ga[:, vs]
            o_ref[0, r0:r0 + t_out, vs] = res[0:t_out].astype(o_ref.dtype)
        for h in range(H_B):
            o = outs_b[h // 2][0][:, (h % 2) * DV_B:(h % 2 + 1) * DV_B]
            vs = slice(MIX_A + h * DV_B, MIX_A + (h + 1) * DV_B)
            res = _rms(o, hn_ref[...]) * gb[:, h * DV_B:(h + 1) * DV_B]
            o_ref[0, r0:r0 + t_out, vs] = res[0:t_out].astype(o_ref.dtype)

    @pl.when(step == pl.num_programs(1) - 1)
    def _():
        for p in range(H_A // 2):
            sg_ref[0, 2 * p] = ya_scr[p, 0:DK_A, 0:DV_A]
            sg_ref[0, 2 * p + 1] = ya_scr[p, DK_A:ka_w, DV_A:PAIR_V]
        for p in range(H_B // 2):
            sh_ref[0, 2 * p] = yb_scr[p, 0:N_B, 0:DV_B]
            sh_ref[0, 2 * p + 1] = yb_scr[p, N_B:kb_w, DV_B:PAIR_V]


def _layer_state_spec(shape, layer, grid_rank):
    tail = (0,) * (len(shape) - 2)
    if grid_rank == 2:
        return pl.BlockSpec((None, 1) + shape[2:], lambda b, s: (layer, b) + tail)
    return pl.BlockSpec((None, 1) + shape[2:], lambda b: (layer, b) + tail)


_ANY_SPEC = pl.BlockSpec(memory_space=pl.ANY)


def _even_mix(pre, sg_in, sh_in, sg_acc, sh_acc, layer, gn, hn, bsz, t_len):
    if t_len % CHUNK == 0:
        c, tb = CHUNK, CHUNK * CHUNKS_PER_STEP
        assert t_len % tb == 0
    else:
        c, tb = SEQ_ROWS, t_len
        assert t_len < c
    pre = [a.reshape(bsz, t_len, a.shape[-1]) for a in pre]
    msum = jnp.asarray(_decay_sum_matrix(c), BF16)
    lev = jnp.asarray(_level_index(c))
    ones_cv = jnp.ones((c, PAIR_V), BF16)
    row = lambda w: pl.BlockSpec((1, tb, w), lambda b, s: (b, s, 0))
    st = lambda a: _layer_state_spec(a.shape, layer, 2)
    scratch = [pltpu.VMEM((H_A // 2, 2 * DK_A, PAIR_V), F32),
               pltpu.VMEM((H_B // 2, 2 * N_B, PAIR_V), F32)]
    if tb < c:
        scratch += [pltpu.VMEM((c, w), F32) for w in EVEN_WIDTHS]
    o_dtype = BF16 if tb >= c else F32
    o, sg, sh = pl.pallas_call(
        functools.partial(_even_mix_kernel, c=c),
        out_shape=[jax.ShapeDtypeStruct((bsz, t_len, D_MODEL), o_dtype),
                   jax.ShapeDtypeStruct(sg_acc.shape, F32),
                   jax.ShapeDtypeStruct(sh_acc.shape, F32)],
        grid=(bsz, t_len // tb),
        in_specs=[_ANY_SPEC, _ANY_SPEC] + [row(w) for w in EVEN_WIDTHS] + [
            _const_spec(msum.shape), _const_spec(lev.shape), _const_spec(ones_cv.shape),
            st(sg_in), st(sh_in), _pspec(gn), _pspec(hn)],
        out_specs=[row(D_MODEL), st(sg_acc), st(sh_acc)],
        scratch_shapes=scratch,
        input_output_aliases={0: 1, 1: 2},
        compiler_params=_cparams(("parallel", "arbitrary")),
        name="even_mix",
    )(sg_acc, sh_acc, *pre, msum, lev, ones_cv, sg_in, sh_in, _parg(gn), _parg(hn))
    return o.reshape(bsz * t_len, D_MODEL), sg, sh


def _rwkv_pre_kernel(*refs, t_len, has_vmix):
    (h_ref, xl_ref, gm_ref, mu_ref, wr_ref, wk_ref, wv_ref, w0_ref, w1_ref, w2_ref,
     a0_ref, a1_ref, a2_ref, g1_ref, g2_ref, kk_ref, ka_ref) = refs[:17]
    pos = 17
    if has_vmix:
        vf_ref, v0_ref, v1_ref, v2_ref = refs[pos:pos + 4]
        pos += 4
    (r_out, lw_out, k_out, v_out, kkr_out, a_out, g_out, xl_out, carry, stage) = refs[pos:]

    tm = h_ref.shape[0]
    step = pl.program_id(0)
    xn = _rms(h_ref[...], gm_ref[...])
    rows = _iota2((tm, 1), 0)
    prev = pltpu.roll(xn, 1, 0)
    if t_len % tm == 0:
        seq = (step * tm) // t_len
        edge = jnp.where((step * tm) % t_len == 0, xl_ref[pl.ds(seq, 1), :], carry[0:1, :])
        prev = jnp.where(rows == 0, edge, prev)
        carry[0:1, :] = xn[tm - 1:tm, :]

        @pl.when((step * tm + tm) % t_len == 0)
        def _():
            xl_out[pl.ds(seq, 1), :] = xn[tm - 1:tm, :]
    else:
        n_seq = tm // t_len
        first_rows = pl.ds(0, n_seq, stride=t_len)
        last_rows = pl.ds(t_len - 1, n_seq, stride=t_len)
        seqs = pl.ds(step * n_seq, n_seq)
        stage[...] = jnp.zeros(stage.shape, F32)
        for j in range(stage.shape[0]):
            stage[j, first_rows, :] = xl_ref[seqs, j * LANE:(j + 1) * LANE]
        spread = jnp.concatenate([stage[j] for j in range(stage.shape[0])], axis=1)
        prev = jnp.where(rows % t_len == 0, spread, prev)
        for j in range(stage.shape[0]):
            stage[j] = xn[:, j * LANE:(j + 1) * LANE]
            xl_out[seqs, j * LANE:(j + 1) * LANE] = stage[j, last_rows, :]
    dx = prev - xn

    def mixed(i):
        return (xn + dx * mu_ref[i:i + 1, :]).astype(BF16)

    _put(r_out, _dot(mixed(0), wr_ref[...]))
    dec = w0_ref[...] + _mm(jnp.tanh(_dot(mixed(1), w1_ref[...])), w2_ref[...])
    _put(lw_out, -_sigmoid(dec) * math.exp(-0.5))
    k = _dot(mixed(2), wk_ref[...])
    xv = mixed(3)
    v = _dot(xv, wv_ref[...])
    a = _sigmoid(a0_ref[...] + _mm(_dot(mixed(4), a1_ref[...]), a2_ref[...]))
    _put(a_out, a)
    _put(g_out, _mm(_sigmoid(_dot(mixed(5), g1_ref[...])), g2_ref[...]))
    _put(kkr_out, k * kk_ref[...])
    _put(k_out, k * (1.0 + (a - 1.0) * ka_ref[...]))
    if has_vmix:
        gate = _sigmoid(v0_ref[...] + _mm(_dot(xv, v1_ref[...]), v2_ref[...]))
        v = v + (vf_ref[...].astype(F32) - v) * gate
    _put(v_out, v)


def _rwkv_pre(h, xl, t_len, wts, vmix, tm):
    m = h.shape[0]
    assert t_len % tm == 0 or tm % t_len == 0
    row = pl.BlockSpec((tm, D_MODEL), lambda i: (i, 0))
    args = [h, xl] + [_parg(w) for w in wts]
    specs = [row, _const_spec(xl.shape)] + [_pspec(w) for w in wts]
    if vmix is not None:
        args += [vmix[0]] + [_parg(w) for w in vmix[1:]]
        specs += [row] + [_pspec(w) for w in vmix[1:]]
    return pl.pallas_call(
        functools.partial(_rwkv_pre_kernel, t_len=t_len, has_vmix=vmix is not None),
        out_shape=[jax.ShapeDtypeStruct((m, D_MODEL), d) for d in RWKV_DTYPES]
        + [jax.ShapeDtypeStruct(xl.shape, F32)],
        grid=(m // tm,),
        in_specs=specs,
        out_specs=[row] * len(RWKV_DTYPES) + [pl.BlockSpec(xl.shape, lambda i: (0, 0))],
        scratch_shapes=[pltpu.VMEM((8, D_MODEL), F32),
                        pltpu.VMEM((D_MODEL // LANE, tm, LANE), F32)],
        compiler_params=_cparams(("arbitrary",)),
        name="rwkv_pre",
    )(*args)


def _mm_pair(x, y, first, x_terms=2, y_terms=2):
    m = x.shape[0]
    xh = x.astype(BF16)
    yh = y.astype(BF16)
    ybh = _stack_heads(yh, first)
    if x_terms == 2:
        both = _dot(jnp.concatenate([xh, (x - xh.astype(F32)).astype(BF16)], axis=0), ybh)
        acc = both[0:m] + both[m:2 * m]
    else:
        acc = _dot(xh, ybh)
    if y_terms == 2:
        acc = acc + _dot(xh, _stack_heads((y - yh.astype(F32)).astype(BF16), first))
    return acc


def _segsum(x, seg):
    c = x.shape[0]
    n = D_MODEL // SEG
    xs = jnp.concatenate([x[:, i * SEG:(i + 1) * SEG] for i in range(n)], axis=0)
    ys = _sum01_right(xs, seg)
    return jnp.concatenate([ys[i * c:(i + 1) * c] for i in range(n)], axis=1)


def _rwkv_head_out(o, r, k, v, g, rk_ref, lnw_ref, lnb_ref, seg):
    mean = _segsum(o, seg) * (1.0 / N_C)
    var = _segsum(jnp.square(o - mean), seg) * (1.0 / N_C)
    on = (o - mean) * lax.rsqrt(var + GN_EPS) * lnw_ref[...] + lnb_ref[...]
    bonus = _segsum(r * k * rk_ref[...], seg) * v
    return (on + bonus) * g


def _rwkv_pair(a_t, r_t, b_t, k_t, b_e, k_e, v, w_tot, y_ref, out, c):
    t_i = _iota2((c, LANE), 0)
    s_i = _iota2((c, LANE), 1) & (c - 1)
    first = _iota2((1, LANE), 1) < N_C
    strict = s_i < t_i
    incl = s_i <= t_i
    eye = (s_i == t_i).astype(F32)
    same_blk = (s_i // SUB) == (t_i // SUB)

    lhs = jnp.concatenate([a_t, r_t], axis=0).astype(BF16)
    rhs = jnp.concatenate([_stack_heads(b_t.astype(BF16), first),
                           _stack_heads(k_t.astype(BF16), first)], axis=0)
    m1 = _dot_nt(lhs, rhs)
    l_ab = jnp.where(strict, m1[0:c, 0:LANE], 0.0).astype(BF16).astype(F32)
    l_ak = jnp.where(strict, m1[0:c, LANE:2 * LANE], 0.0)
    m_rb = jnp.where(incl, m1[c:2 * c, 0:LANE], 0.0)
    m_rk = jnp.where(incl, m1[c:2 * c, LANE:2 * LANE], 0.0)
    y0 = y_ref[...]
    xs = _dot_nt(lhs, y0.astype(BF16))
    v_bd = _stack_heads(v.astype(BF16), first)
    rhs0 = xs[0:c] + _dot(l_ak.astype(BF16), v_bd)
    yield
    l_bd = jnp.where(same_blk, l_ab, 0.0)
    l_off = l_ab - l_bd
    l2 = _mm_pair(l_bd, l_bd, first, 1, 1)
    p = eye - l_bd
    yield
    both = _mm_pair(jnp.concatenate([p, l2], axis=0), l2, first)
    p = p + both[0:c]
    l4 = both[c:2 * c]
    yield
    both = _mm_pair(jnp.concatenate([p, l4], axis=0), l4, first)
    p = p + both[0:c]
    l8 = both[c:2 * c]
    yield
    p = p + _mm_pair(p, l8, first)
    yield
    n = _mm_pair(p, l_off, first, 2, 1)
    yield
    n2 = _mm_pair(n, n, first)
    q = eye - n
    yield
    q = q + _mm_pair(q, n2, first)
    yield
    w = _mm_pair(p, rhs0, first, 2, 1)
    yield
    u = -_mm_pair(q, w, first, 2, 1)
    yield
    mr = jnp.concatenate([m_rb, m_rk], axis=1).astype(BF16)
    uv_bd = jnp.concatenate([_stack_heads(u.astype(BF16), first), v_bd], axis=0)
    out.append(xs[c:2 * c] + _dot(mr, uv_bd))
    uv = jnp.concatenate([u, v], axis=0).astype(BF16)
    bk = jnp.concatenate([b_e, k_e], axis=0).astype(BF16)
    same_head = (_iota2((LANE, 1), 0) // N_C) == (_iota2((1, LANE), 1) // N_C)
    y_ref[...] = y0 * jnp.exp(w_tot) + jnp.where(same_head, _dot_tn(uv, bk), 0.0)


def _rwkv_mix_kernel(s_acc_ref, r_ref, lw_ref, k_ref, v_ref, kkr_ref, a_ref, g_ref, s0_ref, rk_ref,
                     lnw_ref, lnb_ref, tri_ref, seg_ref, o_ref, s_out_ref, y_scr, *, c):
    del s_acc_ref
    step = pl.program_id(1)
    n_pair = H_C // 2

    @pl.when(step == 0)
    def _():
        y_scr[...] = jnp.zeros(y_scr.shape, F32)
        for p in range(n_pair):
            y_scr[p, 0:N_C, 0:N_C] = s0_ref[0, 2 * p]
            y_scr[p, N_C:LANE, N_C:LANE] = s0_ref[0, 2 * p + 1]

    ins = (r_ref, lw_ref, k_ref, v_ref, kkr_ref, a_ref, g_ref)
    seg = seg_ref[...]
    for r0 in range(0, o_ref.shape[1], c):
        r, lw, k, v, kk, a, g = [x[0, r0:r0 + c, :].astype(F32) for x in ins]
        kk = kk * lax.rsqrt(jnp.maximum(_segsum(kk * kk, seg), KK_EPS))
        cw = _sum01_left(tri_ref[...], lw)
        w_tot = cw[c - 1:c]
        kka = kk * a
        a_t = kk * jnp.exp(cw - lw)
        r_t = r * jnp.exp(cw)
        e_inv = jnp.exp(-cw)
        b_t = kka * e_inv
        k_t = k * e_inv
        e_end = jnp.exp(w_tot - cw)
        b_e = kka * e_end
        k_e = k * e_end

        outs = [[] for _ in range(n_pair)]
        gens = []
        for p in range(n_pair):
            sl = slice(p * LANE, (p + 1) * LANE)
            gens.append(_rwkv_pair(a_t[:, sl], r_t[:, sl], b_t[:, sl], k_t[:, sl], b_e[:, sl],
                                   k_e[:, sl], v[:, sl], w_tot[:, sl], y_scr.at[p], outs[p], c))
        _lockstep(gens)
        o = jnp.concatenate([outs[p][0] for p in range(n_pair)], axis=1)
        res = _rwkv_head_out(o, r, k, v, g, rk_ref, lnw_ref, lnb_ref, seg)
        o_ref[0, r0:r0 + c, :] = res.astype(o_ref.dtype)

    @pl.when(step == pl.num_programs(1) - 1)
    def _():
        for p in range(n_pair):
            s_out_ref[0, 2 * p] = y_scr[p, 0:N_C, 0:N_C]
            s_out_ref[0, 2 * p + 1] = y_scr[p, N_C:LANE, N_C:LANE]


def _rwkv_mix(pre, s_in, s_acc, layer, rk, lnw, lnb, bsz, t_len):
    c = CHUNK
    tb = c * CHUNKS_PER_STEP
    assert 2 * c == LANE and c // SUB == 4 and t_len % tb == 0
    pre = [a.reshape(bsz, t_len, D_MODEL) for a in pre]
    tri = jnp.asarray(np.tril(np.ones((c, c), np.float32)), BF16)
    seg = jnp.asarray(np.kron(np.eye(SEG // N_C, dtype=np.float32),
                              np.ones((N_C, N_C), np.float32)), BF16)
    row = pl.BlockSpec((1, tb, D_MODEL), lambda b, s: (b, s, 0))
    st = _layer_state_spec(s_in.shape, layer, 2)
    scratch = [pltpu.VMEM((H_C // 2, LANE, LANE), F32)]
    o, s_new = pl.pallas_call(
        functools.partial(_rwkv_mix_kernel, c=c),
        out_shape=[jax.ShapeDtypeStruct((bsz, t_len, D_MODEL), BF16),
                   jax.ShapeDtypeStruct(s_acc.shape, F32)],
        grid=(bsz, t_len // tb),
        in_specs=[_ANY_SPEC] + [row] * 7 + [st, _pspec(rk), _pspec(lnw), _pspec(lnb),
                                            _const_spec(tri.shape), _const_spec(seg.shape)],
        out_specs=[row, st],
        scratch_shapes=scratch,
        input_output_aliases={0: 1},
        compiler_params=_cparams(("parallel", "arbitrary")),
        name="rwkv_mix",
    )(s_acc, *pre, s_in, _parg(rk), _parg(lnw), _parg(lnb), tri, seg)
    return o.reshape(bsz * t_len, D_MODEL), s_new


def _rwkv_lane_kernel(s_acc_ref, r_ref, lw_ref, k_ref, v_ref, kkr_ref, a_ref, g_ref, s0_ref, rk_ref,
                      lnw_ref, lnb_ref, o_ref, s_out_ref, v_scr, o_scr):
    del s_acc_ref
    t_len = r_ref.shape[0]
    for t in range(t_len):
        kk = kkr_ref[t].astype(F32)
        kk = kk * lax.rsqrt(jnp.maximum(jnp.sum(kk * kk, axis=0, keepdims=True), KK_EPS))
        kka = kk * a_ref[t].astype(F32)
        w = jnp.exp(lw_ref[t])
        k = k_ref[t].astype(F32)
        r = r_ref[t].astype(F32)
        v = v_ref[t].astype(F32)
        v_scr[...] = v
        src = s0_ref if t == 0 else s_out_ref

        def value_row(i, carry, src=src, kk=kk, kka=kka, w=w, k=k, r=r):
            s_i = src[i]
            s_kk = jnp.sum(s_i * kk, axis=0, keepdims=True)
            s_i = s_i * w - s_kk * kka + v_scr[pl.ds(i, 1), :] * k
            s_out_ref[i] = s_i
            o_scr[pl.ds(i, 1), :] = jnp.sum(s_i * r, axis=0, keepdims=True)
            return carry

        lax.fori_loop(0, N_C, value_row, 0, unroll=4)
        o = o_scr[...]
        mean = jnp.mean(o, axis=0, keepdims=True)
        var = jnp.mean(jnp.square(o - mean), axis=0, keepdims=True)
        on = (o - mean) * lax.rsqrt(var + GN_EPS) * lnw_ref[...] + lnb_ref[...]
        bonus = jnp.sum(r * k * rk_ref[...], axis=0, keepdims=True) * v
        o_ref[t] = ((on + bonus) * g_ref[t].astype(F32)).astype(o_ref.dtype)


def _rwkv_lane(pre, s_in, s_acc, layer, rk, lnw, lnb, bsz, t_len):
    assert bsz % LANE == 0 and t_len < CHUNK
    to_lanes = lambda a: a.reshape(bsz, t_len, D_MODEL).transpose(1, 2, 0)
    pre = [to_lanes(a) for a in pre]
    chan = lambda p: jnp.broadcast_to(p[0][p[1]].reshape(D_MODEL, 1), (D_MODEL, bsz))
    tok = pl.BlockSpec((t_len, N_C, bsz), lambda h: (0, h, 0))
    par = pl.BlockSpec((N_C, bsz), lambda h: (h, 0))
    st = pl.BlockSpec((None, None) + s_in.shape[2:], lambda h: (layer, h, 0, 0, 0))
    o, s_new = pl.pallas_call(
        _rwkv_lane_kernel,
        out_shape=[jax.ShapeDtypeStruct((t_len, D_MODEL, bsz), BF16),
                   jax.ShapeDtypeStruct(s_acc.shape, F32)],
        grid=(H_C,),
        in_specs=[_ANY_SPEC] + [tok] * 7 + [st, par, par, par],
        out_specs=[tok, st],
        scratch_shapes=[pltpu.VMEM((N_C, bsz), F32), pltpu.VMEM((N_C, bsz), F32)],
        input_output_aliases={0: 1},
        compiler_params=_cparams(("parallel",)),
        name="rwkv_lane",
    )(s_acc, *pre, s_in, chan(rk), chan(lnw), chan(lnb))
    return o.transpose(2, 0, 1).reshape(bsz * t_len, D_MODEL), s_new


def _post_kernel(h_ref, o_ref, wo_ref, gf_ref, wg_ref, wu_ref, wd_ref, gp_ref, wpg_ref,
                 p_ref, wple_ref, gfin_ref, out_ref, *, final):
    h1 = h_ref[...] + _dot(o_ref[...].astype(BF16), wo_ref[...])
    x2 = _rms(h1, gf_ref[...]).astype(BF16)
    h2 = h1
    for lo in range(0, FFN_DIM, FFN_SPLIT):
        gate = _dot(x2, wg_ref[:, lo:lo + FFN_SPLIT])
        up = _dot(x2, wu_ref[:, lo:lo + FFN_SPLIT])
        act = (_silu(gate) * up).astype(BF16)
        h2 = h2 + _dot(act, wd_ref[lo:lo + FFN_SPLIT, :])
    x3 = _rms(h2, gp_ref[...]).astype(BF16)
    gate = _sigmoid(_dot(x3, wpg_ref[...]))
    emb = _dot(p_ref[...].astype(BF16), wple_ref[...])
    h3 = h2 + gate * emb
    out_ref[...] = _rms(h3, gfin_ref[...]) if final else h3


def _post(h, o, p, layer, wts, final, tm):
    m = h.shape[0]
    wo, gf, wg, wu, wd, gp, wpg, wple, gfin = wts
    row = lambda w: pl.BlockSpec((tm, w), lambda i: (i, 0))
    p_spec = pl.BlockSpec((None, tm, PLE_DIM), lambda i: (layer, i, 0))
    return pl.pallas_call(
        functools.partial(_post_kernel, final=final),
        out_shape=jax.ShapeDtypeStruct((m, D_MODEL), F32),
        grid=(m // tm,),
        in_specs=[row(D_MODEL), row(D_MODEL), _pspec(wo), _pspec(gf), _pspec(wg), _pspec(wu),
                  _pspec(wd), _pspec(gp), _pspec(wpg), p_spec, _pspec(wple), _pspec(gfin)],
        out_specs=row(D_MODEL),
        compiler_params=_cparams(("parallel",)),
        name="post",
    )(h, o, *[_parg(w) for w in (wo, gf, wg, wu, wd, gp, wpg)], p, _parg(wple), _parg(gfin))


def _run_group(x, p, s_gla, s_hgrn, s_rwkv, s_shift, wt):
    bsz, t_len, _ = x.shape
    m = bsz * t_len
    tm = min(512, m)
    h = x.reshape(m, D_MODEL)
    p = p.reshape(DEPTH, m, PLE_DIM)
    long_seq = t_len % CHUNK == 0
    if not long_seq:
        s_rwkv = s_rwkv.transpose(0, 2, 3, 4, 1)
    acc_gla = jnp.zeros(s_gla.shape, F32)
    acc_hgrn = jnp.zeros(s_hgrn.shape, F32)
    acc_rwkv = jnp.zeros(s_rwkv.shape, F32)
    out_shift = []
    v_first = None
    par = lambda name, layer: (wt[name], layer)
    for i in range(DEPTH):
        j = i // 2
        gm = par('norm_mix', i)
        if i % 2 == 0:
            pre = _even_pre(h, [gm, par('w_cat', j), par('w_gk1', j), par('w_gk2', j),
                                par('b_gk', j), wt['hgrn_gamma']], j, tm)
            o, acc_gla, acc_hgrn = _even_mix(pre, s_gla, s_hgrn, acc_gla, acc_hgrn, j,
                                             par('gla_norm', j), par('hgrn_norm', j), bsz, t_len)
            w_o = par('w_out_even', j)
        else:
            wts = [gm] + [par(n, j) for n in ('rw_mu', 'rw_wr', 'rw_wk', 'rw_wv', 'rw_w0', 'rw_w1',
                                               'rw_w2', 'rw_a0', 'rw_a1', 'rw_a2', 'rw_g1', 'rw_g2',
                                               'rw_kk', 'rw_ka')]
            vmix = None
            if j > 0:
                vmix = [v_first] + [par(n, j - 1) for n in ('rw_v0', 'rw_v1', 'rw_v2')]
            r, lw, k, v, kkr, a, g, x_last = _rwkv_pre(h, s_shift[j], t_len, wts, vmix, tm)
            if j == 0:
                v_first = v
            rwkv = _rwkv_mix if long_seq else _rwkv_lane
            o, acc_rwkv = rwkv((r, lw, k, v, kkr, a, g), s_rwkv, acc_rwkv, j, par('rw_rk', j),
                               par('rw_lnw', j), par('rw_lnb', j), bsz, t_len)
            out_shift.append(x_last)
            w_o = par('rw_wo', j)
        post_w = [w_o] + [par(n, i) for n in ('norm_ffn', 'w_ffn_gate', 'w_ffn_up', 'w_ffn_down',
                                              'norm_ple', 'w_ple_gate', 'w_ple')] + [wt['norm_final']]
        h = _post(h, o, p, i, post_w, i == DEPTH - 1, tm)
    if not long_seq:
        acc_rwkv = acc_rwkv.transpose(0, 4, 1, 2, 3)
    return (h.reshape(bsz, t_len, D_MODEL), acc_gla, acc_hgrn, acc_rwkv, jnp.stack(out_shift))


def kernel(x_prompt, x_sample, p_prompt, p_sample, state_gla, state_hgrn, state_rwkv, state_shift, norm_mix, norm_ffn, norm_ple, norm_final, w_in_even, w_gk2, b_gk, gla_norm, hgrn_gamma, hgrn_norm, w_out_even, rw_mu, rw_wr, rw_wk, rw_wv, rw_wo, rw_w0, rw_w1, rw_w2, rw_a0, rw_a1, rw_a2, rw_v0, rw_v1, rw_v2, rw_g1, rw_g2, rw_kk, rw_ka, rw_rk, rw_lnw, rw_lnb, w_ffn_gate, w_ffn_up, w_ffn_down, w_ple, w_ple_gate):
    bf = lambda w: w.astype(BF16)
    w_cat = jnp.concatenate([w_in_even[:, :, 0:1536], w_in_even[:, :, 1552:3600]], axis=-1)
    w_gk1 = jnp.pad(w_in_even[:, :, 1536:1552], ((0, 0), (0, 0), (0, LANE - GK_RANK)))
    w_gk2p = jnp.pad(w_gk2, ((0, 0), (0, LANE - GK_RANK), (0, 0)))
    pad_c = lambda w: jnp.pad(w, ((0, 0), (0, 0), (0, LANE - w.shape[2])))
    pad_r = lambda w: jnp.pad(w, ((0, 0), (0, LANE - w.shape[1]), (0, 0)))
    vec = lambda w: w.reshape(w.shape[0], 1, -1)
    wt = dict(
        norm_mix=vec(norm_mix), norm_ffn=vec(norm_ffn), norm_ple=vec(norm_ple),
        norm_final=norm_final.reshape(1, -1),
        w_cat=bf(w_cat), w_gk1=bf(w_gk1), w_gk2=bf(w_gk2p), b_gk=vec(b_gk),
        gla_norm=vec(gla_norm), hgrn_gamma=hgrn_gamma, hgrn_norm=vec(hgrn_norm),
        w_out_even=bf(w_out_even),
        rw_mu=rw_mu, rw_wr=bf(rw_wr), rw_wk=bf(rw_wk), rw_wv=bf(rw_wv), rw_wo=bf(rw_wo),
        rw_w0=vec(rw_w0), rw_w1=bf(pad_c(rw_w1)), rw_w2=bf(pad_r(rw_w2)),
        rw_a0=vec(rw_a0), rw_a1=bf(pad_c(rw_a1)), rw_a2=bf(pad_r(rw_a2)),
        rw_v0=vec(rw_v0), rw_v1=bf(pad_c(rw_v1)), rw_v2=bf(pad_r(rw_v2)),
        rw_g1=bf(rw_g1), rw_g2=bf(rw_g2), rw_kk=vec(rw_kk), rw_ka=vec(rw_ka), rw_rk=vec(rw_rk),
        rw_lnw=vec(rw_lnw), rw_lnb=vec(rw_lnb),
        w_ffn_gate=bf(w_ffn_gate), w_ffn_up=bf(w_ffn_up), w_ffn_down=bf(w_ffn_down),
        w_ple=bf(w_ple), w_ple_gate=bf(w_ple_gate),
    )
    bp = x_prompt.shape[0]
    zeros = lambda s: jnp.zeros((s.shape[0], bp) + s.shape[2:], F32)
    y_p, gla_p, hgrn_p, rwkv_p, shift_p = _run_group(
        x_prompt, p_prompt, zeros(state_gla), zeros(state_hgrn), zeros(state_rwkv),
        zeros(state_shift), wt)
    y_s, gla_s, hgrn_s, rwkv_s, shift_s = _run_group(
        x_sample, p_sample, state_gla, state_hgrn, state_rwkv, state_shift, wt)
    return (y_p, y_s, gla_p, hgrn_p, rwkv_p, shift_p, gla_s, hgrn_s, rwkv_s, shift_s)
```

```python
import functools
import math

import numpy as np
import jax
import jax.numpy as jnp
from jax import lax
from jax.experimental import pallas as pl
from jax.experimental.pallas import tpu as pltpu

F32 = jnp.float32
BF16 = jnp.bfloat16

D_MODEL = 1024
DEPTH = 4
MIX_A = 512
H_A = 4
DV_A = 128
DK_A = 64
GK_RANK = 16
GK_NORM = 16.0
H_B = 4
DV_B = 128
N_B = 128
LB_FLOOR = 1e-20
N_C = 64
H_C = D_MODEL // N_C
FFN_DIM = 2816
PLE_DIM = 256
RMS_EPS = 1e-6
GN_EPS = 64e-5
KK_EPS = 1e-24

LANE = 128
CHUNK = 64
CHUNKS_PER_STEP = 4
SUB = 16
VMEM_LIMIT = 56 * 1024 * 1024
FFN_SPLIT = 1408
PAIR_V = 2 * DV_A
SEG = 256
SEQ_ROWS = 16

EVEN_WIDTHS = (256, 256, 512, 512, 256, 512, 512, 512, 512, 512)
EVEN_DTYPES = (BF16, BF16, BF16, BF16, F32, BF16, BF16, BF16, BF16, F32)
RWKV_DTYPES = (BF16, F32, BF16, BF16, BF16, BF16, BF16)


def _cparams(sem):
    return pltpu.CompilerParams(dimension_semantics=sem, vmem_limit_bytes=VMEM_LIMIT)


def _const_spec(shape):
    nd = len(shape)
    return pl.BlockSpec(shape, lambda *_: (0,) * nd, pipeline_mode=pl.Buffered(1))


def _pspec(p):
    if not isinstance(p, tuple):
        return _const_spec(p.shape)
    a, layer = p
    nd = a.ndim - 1
    return pl.BlockSpec((None,) + a.shape[1:], lambda *_: (layer,) + (0,) * nd,
                        pipeline_mode=pl.Buffered(1))


def _parg(p):
    return p[0] if isinstance(p, tuple) else p


def _rms(x, g, eps=RMS_EPS):
    return x * lax.rsqrt(jnp.mean(x * x, axis=-1, keepdims=True) + eps) * g


def _sigmoid(x):
    return 1.0 / (1.0 + jnp.exp(-x))


def _silu(x):
    return x * _sigmoid(x)


def _log_sigmoid(x):
    return jnp.minimum(x, 0.0) - jnp.log1p(jnp.exp(-jnp.abs(x)))


def _dot(a, b):
    return jnp.dot(a, b, preferred_element_type=F32)


def _dot_nt(a, b):
    return lax.dot_general(a, b, (((1,), (1,)), ((), ())), preferred_element_type=F32)


def _dot_tn(a, b):
    return lax.dot_general(a, b, (((0,), (0,)), ((), ())), preferred_element_type=F32)


def _mm(a, b):
    return _dot(a.astype(BF16), b.astype(BF16))


def _split3(x):
    hi = x.astype(BF16)
    r1 = x - hi.astype(F32)
    mid = r1.astype(BF16)
    return hi, mid, (r1 - mid.astype(F32)).astype(BF16)


def _sum01_left(m01, x):
    hi, mid, lo = _split3(x)
    return _dot(m01, hi) + _dot(m01, mid) + _dot(m01, lo)


def _sum01_right(x, m01):
    n = x.shape[0]
    hi = x.astype(BF16)
    lo = (x - hi.astype(F32)).astype(BF16)
    y = _dot(jnp.concatenate([hi, lo], axis=0), m01)
    return y[0:n] + y[n:2 * n]


def _iota2(shape, axis):
    return lax.broadcasted_iota(jnp.int32, shape, axis)


def _stack_heads(x, first):
    return jnp.concatenate([jnp.where(first, x, 0), jnp.where(first, 0, x)], axis=0)


def _lockstep(gens):
    gens = list(gens)
    while gens:
        alive = []
        for g in gens:
            try:
                next(g)
                alive.append(g)
            except StopIteration:
                pass
        gens = alive


def _load_rows(ref, scr, c):
    t = ref.shape[1]
    if t == c:
        return ref[0]
    scr[...] = jnp.zeros(scr.shape, F32)
    scr[0:t, :] = ref[0].astype(F32)
    return scr[...]


def _put(ref, val):
    ref[...] = val.astype(ref.dtype)


def _even_pre_kernel(h_ref, gm_ref, w_ref, wgk1_ref, wgk2_ref, bgk_ref, gam_ref,
                     qa_ref, ka_ref, va_ref, ga_ref, lga_ref,
                     qb_ref, kb_ref, vb_ref, gb_ref, lgb_ref, *, layer):
    xn = _rms(h_ref[...], gm_ref[...]).astype(BF16)

    def proj(lo, hi):
        return _dot(xn, w_ref[:, lo:hi])

    _put(qa_ref, proj(0, 256) * DK_A ** -0.5)
    _put(ka_ref, proj(256, 512))
    _put(va_ref, proj(512, 1024))
    _put(ga_ref, _silu(proj(1024, 1536)))
    gk_lr = _dot(xn, wgk1_ref[...])
    gk = _mm(gk_lr, wgk2_ref[...]) + bgk_ref[...]
    _put(lga_ref, _log_sigmoid(gk) / GK_NORM)

    gam = gam_ref[...]
    n_even = gam.shape[0]
    gmax = gam[0:1]
    for i in range(1, n_even):
        gmax = jnp.maximum(gmax, gam[i:i + 1])
    es = [jnp.exp(gam[i:i + 1] - gmax) for i in range(n_even)]
    den = es[0]
    for i in range(1, n_even):
        den = den + es[i]
    sm = [e / den for e in es]
    cum = sm[0]
    for i in range(1, layer + 1):
        cum = cum + sm[i]
    lb = cum - sm[0]

    _put(qb_ref, _silu(proj(1536, 2048)) * N_B ** -0.5)
    z = proj(2048, 2560)
    _put(lgb_ref, jnp.log(jnp.maximum(lb, LB_FLOOR) + (1.0 - lb) * _sigmoid(z)))
    _put(kb_ref, (1.0 - lb) * _sigmoid(-z))
    _put(vb_ref, proj(2560, 3072))
    _put(gb_ref, _silu(proj(3072, 3584)))


def _even_pre(h, params, layer, tm):
    m = h.shape[0]
    row = lambda w: pl.BlockSpec((tm, w), lambda i: (i, 0))
    return pl.pallas_call(
        functools.partial(_even_pre_kernel, layer=layer),
        out_shape=[jax.ShapeDtypeStruct((m, w), d) for w, d in zip(EVEN_WIDTHS, EVEN_DTYPES)],
        grid=(m // tm,),
        in_specs=[row(D_MODEL)] + [_pspec(p) for p in params],
        out_specs=[row(w) for w in EVEN_WIDTHS],
        compiler_params=_cparams(("parallel",)),
        name="even_pre",
    )(h, *[_parg(p) for p in params])


def _level_widths(c):
    return [c >> i for i in range(1, int(math.log2(c)) + 1)]


def _decay_sum_matrix(c):
    i = np.arange(c)[:, None]
    t = np.arange(c)[None, :]
    blocks = [t <= i, t > i]
    for w in _level_widths(c):
        ref = (i // (2 * w)) * (2 * w) + w
        blocks.append(np.where(i >= ref, (t > ref) & (t <= i), (t > i) & (t <= ref)))
    return np.concatenate(blocks, axis=0).astype(np.float32)


def _level_index(c):
    t = np.arange(c)[:, None]
    s = np.arange(c)[None, :]
    lev = np.full((c, c), -1, np.int32)
    lev[t == s] = 0
    for li, w in enumerate(_level_widths(c)):
        m = (t // (2 * w) == s // (2 * w)) & (t % (2 * w) >= w) & (s % (2 * w) < w)
        lev[m] = li + 1
    return np.tile(lev, (1, 2))


def _row_to_columns(row, ones_rows):
    x = jnp.broadcast_to(row, (ones_rows.shape[0], row.shape[1]))
    hi = x.astype(BF16).astype(F32)
    r1 = x - hi
    mid = r1.astype(BF16).astype(F32)
    lo = (r1 - mid).astype(BF16).astype(F32)
    ridx = _iota2(x.shape, 0)
    terms = jnp.where(ridx == 0, hi, jnp.where(ridx == 1, mid, jnp.where(ridx == 2, lo, 0.0)))
    return _dot_tn(terms.astype(BF16), ones_rows)


def _gated_pair(q, k, v, e_all, lev, ones_rows, y_ref, out, c):
    kw = q.shape[1]
    nlev = len(_level_widths(c))
    first_k = _iota2((1, kw), 1) < kw // 2
    first_v = _iota2((1, PAIR_V), 1) < PAIR_V // 2
    parts = []
    for l in range(nlev + 1):
        if l == 0:
            q_l, k_l = q, k
        else:
            e = e_all[(l + 1) * c:(l + 2) * c]
            q_l, k_l = q * e, k * e
        parts.append(_dot_nt(q_l.astype(BF16), _stack_heads(k_l.astype(BF16), first_k)))
    y0 = y_ref[...]
    o_inter = _mm(q * e_all[0:c], y0)
    upd = _dot_tn((k * e_all[c:2 * c]).astype(BF16), v.astype(BF16))
    decay = _row_to_columns(e_all[c - 1:c], ones_rows)
    yield
    a_mat = jnp.where(lev == 0, parts[0], 0.0)
    for l in range(1, nlev + 1):
        a_mat = jnp.where(lev == l, parts[l], a_mat)
    o = o_inter + _dot(a_mat.astype(BF16), _stack_heads(v.astype(BF16), first_v))
    same_head = (_iota2((kw, 1), 0) // (kw // 2)) == (_iota2((1, PAIR_V), 1) // (PAIR_V // 2))
    y_ref[...] = decay * y0 + jnp.where(same_head, upd, 0.0)
    out.append(o)


def _even_mix_kernel(sg_acc_ref, sh_acc_ref, qa_ref, ka_ref, va_ref, ga_ref, lga_ref, qb_ref,
                     kb_ref, vb_ref, gb_ref, lgb_ref, msum_ref, lev_ref, ones_ref, sg0_ref,
                     sh0_ref, gn_ref, hn_ref, o_ref, sg_ref, sh_ref, ya_scr, yb_scr, *pads, c):
    del sg_acc_ref, sh_acc_ref
    step = pl.program_id(1)
    ka_w = 2 * DK_A
    kb_w = 2 * N_B

    @pl.when(step == 0)
    def _():
        ya_scr[...] = jnp.zeros(ya_scr.shape, F32)
        yb_scr[...] = jnp.zeros(yb_scr.shape, F32)
        for p in range(H_A // 2):
            ya_scr[p, 0:DK_A, 0:DV_A] = sg0_ref[0, 2 * p]
            ya_scr[p, DK_A:ka_w, DV_A:PAIR_V] = sg0_ref[0, 2 * p + 1]
        for p in range(H_B // 2):
            yb_scr[p, 0:N_B, 0:DV_B] = sh0_ref[0, 2 * p]
            yb_scr[p, N_B:kb_w, DV_B:PAIR_V] = sh0_ref[0, 2 * p + 1]

    ins = (qa_ref, ka_ref, va_ref, ga_ref, lga_ref, qb_ref, kb_ref, vb_ref, gb_ref, lgb_ref)
    msum = msum_ref[...]
    lev = lev_ref[...]
    ones_cv = ones_ref[...]
    rows = o_ref.shape[1]
    for r0 in range(0, rows, c):
        if pads:
            vals = [_load_rows(r, s, c) for r, s in zip(ins, pads)]
        else:
            vals = [r[0, r0:r0 + c, :].astype(F32) for r in ins]
        qa, ka, va, ga, lga, qb, kb, vb, gb, lgb = vals
        t_out = min(c, rows)
        ea = jnp.exp(_sum01_left(msum, lga))
        eb = jnp.exp(_sum01_left(msum, lgb))

        outs_a = [[] for _ in range(H_A // 2)]
        outs_b = [[] for _ in range(H_B // 2)]
        gens = []
        for p in range(H_A // 2):
            ks = slice(p * ka_w, (p + 1) * ka_w)
            vs = slice(p * PAIR_V, (p + 1) * PAIR_V)
            gens.append(_gated_pair(qa[:, ks], ka[:, ks], va[:, vs], ea[:, ks], lev,
                                    ones_cv, ya_scr.at[p], outs_a[p], c))
        for p in range(H_B // 2):
            ks = slice(p * kb_w, (p + 1) * kb_w)
            vs = slice(p * PAIR_V, (p + 1) * PAIR_V)
            gens.append(_gated_pair(qb[:, ks], kb[:, ks], vb[:, vs], eb[:, ks], lev,
                                    ones_cv, yb_scr.at[p], outs_b[p], c))
        _lockstep(gens)

        for h in range(H_A):
            o = outs_a[h // 2][0][:, (h % 2) * DV_A:(h % 2 + 1) * DV_A]
            vs = slice(h * DV_A, (h + 1) * DV_A)
            res = _rms(o, gn_ref[...]) * ga[:, vs]
            o_ref[0, r0:r0 + t_out, vs] = res[0:t_out].astype(o_ref.dtype)
        for h in range(H_B):
            o = outs_b[h // 2][0][:, (h % 2) * DV_B:(h % 2 + 1) * DV_B]
            vs = slice(MIX_A + h * DV_B, MIX_A + (h + 1) * DV_B)
            res = _rms(o, hn_ref[...]) * gb[:, h * DV_B:(h + 1) * DV_B]
            o_ref[0, r0:r0 + t_out, vs] = res[0:t_out].astype(o_ref.dtype)

    @pl.when(step == pl.num_programs(1) - 1)
    def _():
        for p in range(H_A // 2):
            sg_ref[0, 2 * p] = ya_scr[p, 0:DK_A, 0:DV_A]
            sg_ref[0, 2 * p + 1] = ya_scr[p, DK_A:ka_w, DV_A:PAIR_V]
        for p in range(H_B // 2):
            sh_ref[0, 2 * p] = yb_scr[p, 0:N_B, 0:DV_B]
            sh_ref[0, 2 * p + 1] = yb_scr[p, N_B:kb_w, DV_B:PAIR_V]


def _layer_state_spec(shape, layer, grid_rank):
    tail = (0,) * (len(shape) - 2)
    if grid_rank == 2:
        return pl.BlockSpec((None, 1) + shape[2:], lambda b, s: (layer, b) + tail)
    return pl.BlockSpec((None, 1) + shape[2:], lambda b: (layer, b) + tail)


_ANY_SPEC = pl.BlockSpec(memory_space=pl.ANY)


def _even_mix(pre, sg_in, sh_in, sg_acc, sh_acc, layer, gn, hn, bsz, t_len):
    if t_len % CHUNK == 0:
        c, tb = CHUNK, CHUNK * CHUNKS_PER_STEP
        assert t_len % tb == 0
    else:
        c, tb = SEQ_ROWS, t_len
        assert t_len < c
    pre = [a.reshape(bsz, t_len, a.shape[-1]) for a in pre]
    msum = jnp.asarray(_decay_sum_matrix(c), BF16)
    lev = jnp.asarray(_level_index(c))
    ones_cv = jnp.ones((SEQ_ROWS, PAIR_V), BF16)
    row = lambda w: pl.BlockSpec((1, tb, w), lambda b, s: (b, s, 0))
    st = lambda a: _layer_state_spec(a.shape, layer, 2)
    scratch = [pltpu.VMEM((H_A // 2, 2 * DK_A, PAIR_V), F32),
               pltpu.VMEM((H_B // 2, 2 * N_B, PAIR_V), F32)]
    if tb < c:
        scratch += [pltpu.VMEM((c, w), F32) for w in EVEN_WIDTHS]
    o_dtype = BF16 if tb >= c else F32
    o, sg, sh = pl.pallas_call(
        functools.partial(_even_mix_kernel, c=c),
        out_shape=[jax.ShapeDtypeStruct((bsz, t_len, D_MODEL), o_dtype),
                   jax.ShapeDtypeStruct(sg_acc.shape, F32),
                   jax.ShapeDtypeStruct(sh_acc.shape, F32)],
        grid=(bsz, t_len // tb),
        in_specs=[_ANY_SPEC, _ANY_SPEC] + [row(w) for w in EVEN_WIDTHS] + [
            _const_spec(msum.shape), _const_spec(lev.shape), _const_spec(ones_cv.shape),
            st(sg_in), st(sh_in), _pspec(gn), _pspec(hn)],
        out_specs=[row(D_MODEL), st(sg_acc), st(sh_acc)],
        scratch_shapes=scratch,
        input_output_aliases={0: 1, 1: 2},
        compiler_params=_cparams(("parallel", "arbitrary")),
        name="even_mix",
    )(sg_acc, sh_acc, *pre, msum, lev, ones_cv, sg_in, sh_in, _parg(gn), _parg(hn))
    return o.reshape(bsz * t_len, D_MODEL), sg, sh


def _rwkv_pre_kernel(*refs, t_len, has_vmix):
    (h_ref, xl_ref, gm_ref, mu_ref, wr_ref, wk_ref, wv_ref, w0_ref, w1_ref, w2_ref,
     a0_ref, a1_ref, a2_ref, g1_ref, g2_ref, kk_ref, ka_ref) = refs[:17]
    pos = 17
    if has_vmix:
        vf_ref, v0_ref, v1_ref, v2_ref = refs[pos:pos + 4]
        pos += 4
    (r_out, lw_out, k_out, v_out, kkr_out, a_out, g_out, xl_out, carry, stage) = refs[pos:]

    tm = h_ref.shape[0]
    step = pl.program_id(0)
    xn = _rms(h_ref[...], gm_ref[...])
    rows = _iota2((tm, 1), 0)
    prev = pltpu.roll(xn, 1, 0)
    if t_len % tm == 0:
        seq = (step * tm) // t_len
        edge = jnp.where((step * tm) % t_len == 0, xl_ref[pl.ds(seq, 1), :], carry[0:1, :])
        prev = jnp.where(rows == 0, edge, prev)
        carry[0:1, :] = xn[tm - 1:tm, :]

        @pl.when((step * tm + tm) % t_len == 0)
        def _():
            xl_out[pl.ds(seq, 1), :] = xn[tm - 1:tm, :]
    else:
        n_seq = tm // t_len
        first_rows = pl.ds(0, n_seq, stride=t_len)
        last_rows = pl.ds(t_len - 1, n_seq, stride=t_len)
        seqs = pl.ds(step * n_seq, n_seq)
        stage[...] = jnp.zeros(stage.shape, F32)
        for j in range(stage.shape[0]):
            stage[j, first_rows, :] = xl_ref[seqs, j * LANE:(j + 1) * LANE]
        spread = jnp.concatenate([stage[j] for j in range(stage.shape[0])], axis=1)
        prev = jnp.where(rows % t_len == 0, spread, prev)
        for j in range(stage.shape[0]):
            stage[j] = xn[:, j * LANE:(j + 1) * LANE]
            xl_out[seqs, j * LANE:(j + 1) * LANE] = stage[j, last_rows, :]
    dx = prev - xn

    def mixed(i):
        return (xn + dx * mu_ref[i:i + 1, :]).astype(BF16)

    _put(r_out, _dot(mixed(0), wr_ref[...]))
    dec = w0_ref[...] + _mm(jnp.tanh(_dot(mixed(1), w1_ref[...])), w2_ref[...])
    _put(lw_out, -_sigmoid(dec) * math.exp(-0.5))
    k = _dot(mixed(2), wk_ref[...])
    xv = mixed(3)
    v = _dot(xv, wv_ref[...])
    a = _sigmoid(a0_ref[...] + _mm(_dot(mixed(4), a1_ref[...]), a2_ref[...]))
    _put(a_out, a)
    _put(g_out, _mm(_sigmoid(_dot(mixed(5), g1_ref[...])), g2_ref[...]))
    _put(kkr_out, k * kk_ref[...])
    _put(k_out, k * (1.0 + (a - 1.0) * ka_ref[...]))
    if has_vmix:
        gate = _sigmoid(v0_ref[...] + _mm(_dot(xv, v1_ref[...]), v2_ref[...]))
        v = v + (vf_ref[...].astype(F32) - v) * gate
    _put(v_out, v)


def _rwkv_pre(h, xl, t_len, wts, vmix, tm):
    m = h.shape[0]
    assert t_len % tm == 0 or tm % t_len == 0
    row = pl.BlockSpec((tm, D_MODEL), lambda i: (i, 0))
    args = [h, xl] + [_parg(w) for w in wts]
    specs = [row, _const_spec(xl.shape)] + [_pspec(w) for w in wts]
    if vmix is not None:
        args += [vmix[0]] + [_parg(w) for w in vmix[1:]]
        specs += [row] + [_pspec(w) for w in vmix[1:]]
    return pl.pallas_call(
        functools.partial(_rwkv_pre_kernel, t_len=t_len, has_vmix=vmix is not None),
        out_shape=[jax.ShapeDtypeStruct((m, D_MODEL), d) for d in RWKV_DTYPES]
        + [jax.ShapeDtypeStruct(xl.shape, F32)],
        grid=(m // tm,),
        in_specs=specs,
        out_specs=[row] * len(RWKV_DTYPES) + [pl.BlockSpec(xl.shape, lambda i: (0, 0))],
        scratch_shapes=[pltpu.VMEM((8, D_MODEL), F32),
                        pltpu.VMEM((D_MODEL // LANE, tm, LANE), F32)],
        compiler_params=_cparams(("arbitrary",)),
        name="rwkv_pre",
    )(*args)


def _mm_pair(x, y, first, x_terms=2, y_terms=2):
    m = x.shape[0]
    xh = x.astype(BF16)
    yh = y.astype(BF16)
    ybh = _stack_heads(yh, first)
    if x_terms == 2:
        both = _dot(jnp.concatenate([xh, (x - xh.astype(F32)).astype(BF16)], axis=0), ybh)
        acc = both[0:m] + both[m:2 * m]
    else:
        acc = _dot(xh, ybh)
    if y_terms == 2:
        acc = acc + _dot(xh, _stack_heads((y - yh.astype(F32)).astype(BF16), first))
    return acc


def _segsum(x, seg):
    c = x.shape[0]
    n = D_MODEL // SEG
    xs = jnp.concatenate([x[:, i * SEG:(i + 1) * SEG] for i in range(n)], axis=0)
    ys = _sum01_right(xs, seg)
    return jnp.concatenate([ys[i * c:(i + 1) * c] for i in range(n)], axis=1)


def _rwkv_head_out(o, r, k, v, g, rk_ref, lnw_ref, lnb_ref, seg):
    mean = _segsum(o, seg) * (1.0 / N_C)
    var = _segsum(jnp.square(o - mean), seg) * (1.0 / N_C)
    on = (o - mean) * lax.rsqrt(var + GN_EPS) * lnw_ref[...] + lnb_ref[...]
    bonus = _segsum(r * k * rk_ref[...], seg) * v
    return (on + bonus) * g


def _rwkv_pair(a_t, r_t, b_t, k_t, b_e, k_e, v, w_tot, y_ref, out, c):
    t_i = _iota2((c, LANE), 0)
    s_i = _iota2((c, LANE), 1) & (c - 1)
    first = _iota2((1, LANE), 1) < N_C
    strict = s_i < t_i
    incl = s_i <= t_i
    eye = (s_i == t_i).astype(F32)
    same_blk = (s_i // SUB) == (t_i // SUB)

    lhs = jnp.concatenate([a_t, r_t], axis=0).astype(BF16)
    rhs = jnp.concatenate([_stack_heads(b_t.astype(BF16), first),
                           _stack_heads(k_t.astype(BF16), first)], axis=0)
    m1 = _dot_nt(lhs, rhs)
    l_ab = jnp.where(strict, m1[0:c, 0:LANE], 0.0).astype(BF16).astype(F32)
    l_ak = jnp.where(strict, m1[0:c, LANE:2 * LANE], 0.0)
    m_rb = jnp.where(incl, m1[c:2 * c, 0:LANE], 0.0)
    m_rk = jnp.where(incl, m1[c:2 * c, LANE:2 * LANE], 0.0)
    y0 = y_ref[...]
    xs = _dot_nt(lhs, y0.astype(BF16))
    v_bd = _stack_heads(v.astype(BF16), first)
    rhs0 = xs[0:c] + _dot(l_ak.astype(BF16), v_bd)
    yield
    l_bd = jnp.where(same_blk, l_ab, 0.0)
    l_off = l_ab - l_bd
    l2 = _mm_pair(l_bd, l_bd, first, 1, 1)
    p = eye - l_bd
    yield
    both = _mm_pair(jnp.concatenate([p, l2], axis=0), l2, first)
    p = p + both[0:c]
    l4 = both[c:2 * c]
    yield
    both = _mm_pair(jnp.concatenate([p, l4], axis=0), l4, first)
    p = p + both[0:c]
    l8 = both[c:2 * c]
    yield
    p = p + _mm_pair(p, l8, first)
    yield
    n = _mm_pair(p, l_off, first, 1, 1)
    yield
    n2 = _mm_pair(n, n, first, 1, 1)
    q = eye - n
    yield
    q = q + _mm_pair(q, n2, first, 1, 1)
    yield
    w = _mm_pair(p, rhs0, first, 1, 1)
    yield
    u = -_mm_pair(q, w, first, 1, 1)
    yield
    mr = jnp.concatenate([m_rb, m_rk], axis=1).astype(BF16)
    uv_bd = jnp.concatenate([_stack_heads(u.astype(BF16), first), v_bd], axis=0)
    out.append(xs[c:2 * c] + _dot(mr, uv_bd))
    uv = jnp.concatenate([u, v], axis=0).astype(BF16)
    bk = jnp.concatenate([b_e, k_e], axis=0).astype(BF16)
    same_head = (_iota2((LANE, 1), 0) // N_C) == (_iota2((1, LANE), 1) // N_C)
    y_ref[...] = y0 * jnp.exp(w_tot) + jnp.where(same_head, _dot_tn(uv, bk), 0.0)


def _rwkv_mix_kernel(s_acc_ref, r_ref, lw_ref, k_ref, v_ref, kkr_ref, a_ref, g_ref, s0_ref, rk_ref,
                     lnw_ref, lnb_ref, tri_ref, seg_ref, o_ref, s_out_ref, y_scr, *, c):
    del s_acc_ref
    step = pl.program_id(1)
    n_pair = H_C // 2

    @pl.when(step == 0)
    def _():
        y_scr[...] = jnp.zeros(y_scr.shape, F32)
        for p in range(n_pair):
            y_scr[p, 0:N_C, 0:N_C] = s0_ref[0, 2 * p]
            y_scr[p, N_C:LANE, N_C:LANE] = s0_ref[0, 2 * p + 1]

    ins = (r_ref, lw_ref, k_ref, v_ref, kkr_ref, a_ref, g_ref)
    seg = seg_ref[...]
    for r0 in range(0, o_ref.shape[1], c):
        r, lw, k, v, kk, a, g = [x[0, r0:r0 + c, :].astype(F32) for x in ins]
        kk = kk * lax.rsqrt(jnp.maximum(_segsum(kk * kk, seg), KK_EPS))
        cw = _sum01_left(tri_ref[...], lw)
        w_tot = cw[c - 1:c]
        kka = kk * a
        a_t = kk * jnp.exp(cw - lw)
        r_t = r * jnp.exp(cw)
        e_inv = jnp.exp(-cw)
        b_t = kka * e_inv
        k_t = k * e_inv
        e_end = jnp.exp(w_tot - cw)
        b_e = kka * e_end
        k_e = k * e_end

        outs = [[] for _ in range(n_pair)]
        gens = []
        for p in range(n_pair):
            sl = slice(p * LANE, (p + 1) * LANE)
            gens.append(_rwkv_pair(a_t[:, sl], r_t[:, sl], b_t[:, sl], k_t[:, sl], b_e[:, sl],
                                   k_e[:, sl], v[:, sl], w_tot[:, sl], y_scr.at[p], outs[p], c))
        _lockstep(gens)
        o = jnp.concatenate([outs[p][0] for p in range(n_pair)], axis=1)
        res = _rwkv_head_out(o, r, k, v, g, rk_ref, lnw_ref, lnb_ref, seg)
        o_ref[0, r0:r0 + c, :] = res.astype(o_ref.dtype)

    @pl.when(step == pl.num_programs(1) - 1)
    def _():
        for p in range(n_pair):
            s_out_ref[0, 2 * p] = y_scr[p, 0:N_C, 0:N_C]
            s_out_ref[0, 2 * p + 1] = y_scr[p, N_C:LANE, N_C:LANE]


def _rwkv_mix(pre, s_in, s_acc, layer, rk, lnw, lnb, bsz, t_len):
    c = CHUNK
    tb = c * CHUNKS_PER_STEP
    assert 2 * c == LANE and c // SUB == 4 and t_len % tb == 0
    pre = [a.reshape(bsz, t_len, D_MODEL) for a in pre]
    tri = jnp.asarray(np.tril(np.ones((c, c), np.float32)), BF16)
    seg = jnp.asarray(np.kron(np.eye(SEG // N_C, dtype=np.float32),
                              np.ones((N_C, N_C), np.float32)), BF16)
    row = pl.BlockSpec((1, tb, D_MODEL), lambda b, s: (b, s, 0))
    st = _layer_state_spec(s_in.shape, layer, 2)
    scratch = [pltpu.VMEM((H_C // 2, LANE, LANE), F32)]
    o, s_new = pl.pallas_call(
        functools.partial(_rwkv_mix_kernel, c=c),
        out_shape=[jax.ShapeDtypeStruct((bsz, t_len, D_MODEL), BF16),
                   jax.ShapeDtypeStruct(s_acc.shape, F32)],
        grid=(bsz, t_len // tb),
        in_specs=[_ANY_SPEC] + [row] * 7 + [st, _pspec(rk), _pspec(lnw), _pspec(lnb),
                                            _const_spec(tri.shape), _const_spec(seg.shape)],
        out_specs=[row, st],
        scratch_shapes=scratch,
        input_output_aliases={0: 1},
        compiler_params=_cparams(("parallel", "arbitrary")),
        name="rwkv_mix",
    )(s_acc, *pre, s_in, _parg(rk), _parg(lnw), _parg(lnb), tri, seg)
    return o.reshape(bsz * t_len, D_MODEL), s_new


def _rwkv_lane_kernel(s_acc_ref, r_ref, lw_ref, k_ref, v_ref, kkr_ref, a_ref, g_ref, s0_ref, rk_ref,
                      lnw_ref, lnb_ref, o_ref, s_out_ref, v_scr, o_scr):
    del s_acc_ref
    t_len = r_ref.shape[0]
    for t in range(t_len):
        kk = kkr_ref[t].astype(F32)
        kk = kk * lax.rsqrt(jnp.maximum(jnp.sum(kk * kk, axis=0, keepdims=True), KK_EPS))
        kka = kk * a_ref[t].astype(F32)
        w = jnp.exp(lw_ref[t])
        k = k_ref[t].astype(F32)
        r = r_ref[t].astype(F32)
        v = v_ref[t].astype(F32)
        v_scr[...] = v
        src = s0_ref if t == 0 else s_out_ref

        def value_row(i, carry, src=src, kk=kk, kka=kka, w=w, k=k, r=r):
            s_i = src[i]
            s_kk = jnp.sum(s_i * kk, axis=0, keepdims=True)
            s_i = s_i * w - s_kk * kka + v_scr[pl.ds(i, 1), :] * k
            s_out_ref[i] = s_i
            o_scr[pl.ds(i, 1), :] = jnp.sum(s_i * r, axis=0, keepdims=True)
            return carry

        lax.fori_loop(0, N_C, value_row, 0, unroll=4)
        o = o_scr[...]
        mean = jnp.mean(o, axis=0, keepdims=True)
        var = jnp.mean(jnp.square(o - mean), axis=0, keepdims=True)
        on = (o - mean) * lax.rsqrt(var + GN_EPS) * lnw_ref[...] + lnb_ref[...]
        bonus = jnp.sum(r * k * rk_ref[...], axis=0, keepdims=True) * v
        o_ref[t] = ((on + bonus) * g_ref[t].astype(F32)).astype(o_ref.dtype)


def _rwkv_lane(pre, s_in, s_acc, layer, rk, lnw, lnb, bsz, t_len):
    assert bsz % LANE == 0 and t_len < CHUNK
    to_lanes = lambda a: a.reshape(bsz, t_len, D_MODEL).transpose(1, 2, 0)
    pre = [to_lanes(a) for a in pre]
    chan = lambda p: jnp.broadcast_to(p[0][p[1]].reshape(D_MODEL, 1), (D_MODEL, bsz))
    tok = pl.BlockSpec((t_len, N_C, bsz), lambda h: (0, h, 0))
    par = pl.BlockSpec((N_C, bsz), lambda h: (h, 0))
    st = pl.BlockSpec((None, None) + s_in.shape[2:], lambda h: (layer, h, 0, 0, 0))
    o, s_new = pl.pallas_call(
        _rwkv_lane_kernel,
        out_shape=[jax.ShapeDtypeStruct((t_len, D_MODEL, bsz), BF16),
                   jax.ShapeDtypeStruct(s_acc.shape, F32)],
        grid=(H_C,),
        in_specs=[_ANY_SPEC] + [tok] * 7 + [st, par, par, par],
        out_specs=[tok, st],
        scratch_shapes=[pltpu.VMEM((N_C, bsz), F32), pltpu.VMEM((N_C, bsz), F32)],
        input_output_aliases={0: 1},
        compiler_params=_cparams(("parallel",)),
        name="rwkv_lane",
    )(s_acc, *pre, s_in, chan(rk), chan(lnw), chan(lnb))
    return o.transpose(2, 0, 1).reshape(bsz * t_len, D_MODEL), s_new


def _post_kernel(h_ref, o_ref, wo_ref, gf_ref, wg_ref, wu_ref, wd_ref, gp_ref, wpg_ref,
                 p_ref, wple_ref, gfin_ref, out_ref, *, final):
    h1 = h_ref[...] + _dot(o_ref[...].astype(BF16), wo_ref[...])
    x2 = _rms(h1, gf_ref[...]).astype(BF16)
    h2 = h1
    for lo in range(0, FFN_DIM, FFN_SPLIT):
        gate = _dot(x2, wg_ref[:, lo:lo + FFN_SPLIT])
        up = _dot(x2, wu_ref[:, lo:lo + FFN_SPLIT])
        act = (_silu(gate) * up).astype(BF16)
        h2 = h2 + _dot(act, wd_ref[lo:lo + FFN_SPLIT, :])
    x3 = _rms(h2, gp_ref[...]).astype(BF16)
    gate = _sigmoid(_dot(x3, wpg_ref[...]))
    emb = _dot(p_ref[...].astype(BF16), wple_ref[...])
    h3 = h2 + gate * emb
    out_ref[...] = _rms(h3, gfin_ref[...]) if final else h3


def _post(h, o, p, layer, wts, final, tm):
    m = h.shape[0]
    wo, gf, wg, wu, wd, gp, wpg, wple, gfin = wts
    row = lambda w: pl.BlockSpec((tm, w), lambda i: (i, 0))
    p_spec = pl.BlockSpec((None, tm, PLE_DIM), lambda i: (layer, i, 0))
    return pl.pallas_call(
        functools.partial(_post_kernel, final=final),
        out_shape=jax.ShapeDtypeStruct((m, D_MODEL), F32),
        grid=(m // tm,),
        in_specs=[row(D_MODEL), row(D_MODEL), _pspec(wo), _pspec(gf), _pspec(wg), _pspec(wu),
                  _pspec(wd), _pspec(gp), _pspec(wpg), p_spec, _pspec(wple), _pspec(gfin)],
        out_specs=row(D_MODEL),
        compiler_params=_cparams(("parallel",)),
        name="post",
    )(h, o, *[_parg(w) for w in (wo, gf, wg, wu, wd, gp, wpg)], p, _parg(wple), _parg(gfin))


def _run_group(x, p, s_gla, s_hgrn, s_rwkv, s_shift, wt):
    bsz, t_len, _ = x.shape
    m = bsz * t_len
    tm = min(512, m)
    h = x.reshape(m, D_MODEL)
    p = p.reshape(DEPTH, m, PLE_DIM)
    long_seq = t_len % CHUNK == 0
    if not long_seq:
        s_rwkv = s_rwkv.transpose(0, 2, 3, 4, 1)
    acc_gla = jnp.zeros(s_gla.shape, F32)
    acc_hgrn = jnp.zeros(s_hgrn.shape, F32)
    acc_rwkv = jnp.zeros(s_rwkv.shape, F32)
    out_shift = []
    v_first = None
    par = lambda name, layer: (wt[name], layer)
    for i in range(DEPTH):
        j = i // 2
        gm = par('norm_mix', i)
        if i % 2 == 0:
            pre = _even_pre(h, [gm, par('w_cat', j), par('w_gk1', j), par('w_gk2', j),
                                par('b_gk', j), wt['hgrn_gamma']], j, tm)
            o, acc_gla, acc_hgrn = _even_mix(pre, s_gla, s_hgrn, acc_gla, acc_hgrn, j,
                                             par('gla_norm', j), par('hgrn_norm', j), bsz, t_len)
            w_o = par('w_out_even', j)
        else:
            wts = [gm] + [par(n, j) for n in ('rw_mu', 'rw_wr', 'rw_wk', 'rw_wv', 'rw_w0', 'rw_w1',
                                               'rw_w2', 'rw_a0', 'rw_a1', 'rw_a2', 'rw_g1', 'rw_g2',
                                               'rw_kk', 'rw_ka')]
            vmix = None
            if j > 0:
                vmix = [v_first] + [par(n, j - 1) for n in ('rw_v0', 'rw_v1', 'rw_v2')]
            r, lw, k, v, kkr, a, g, x_last = _rwkv_pre(h, s_shift[j], t_len, wts, vmix, tm)
            if j == 0:
                v_first = v
            rwkv = _rwkv_mix if long_seq else _rwkv_lane
            o, acc_rwkv = rwkv((r, lw, k, v, kkr, a, g), s_rwkv, acc_rwkv, j, par('rw_rk', j),
                               par('rw_lnw', j), par('rw_lnb', j), bsz, t_len)
            out_shift.append(x_last)
            w_o = par('rw_wo', j)
        post_w = [w_o] + [par(n, i) for n in ('norm_ffn', 'w_ffn_gate', 'w_ffn_up', 'w_ffn_down',
                                              'norm_ple', 'w_ple_gate', 'w_ple')] + [wt['norm_final']]
        h = _post(h, o, p, i, post_w, i == DEPTH - 1, tm)
    if not long_seq:
        acc_rwkv = acc_rwkv.transpose(0, 4, 1, 2, 3)
    return (h.reshape(bsz, t_len, D_MODEL), acc_gla, acc_hgrn, acc_rwkv, jnp.stack(out_shift))


def kernel(x_prompt, x_sample, p_prompt, p_sample, state_gla, state_hgrn, state_rwkv, state_shift, norm_mix, norm_ffn, norm_ple, norm_final, w_in_even, w_gk2, b_gk, gla_norm, hgrn_gamma, hgrn_norm, w_out_even, rw_mu, rw_wr, rw_wk, rw_wv, rw_wo, rw_w0, rw_w1, rw_w2, rw_a0, rw_a1, rw_a2, rw_v0, rw_v1, rw_v2, rw_g1, rw_g2, rw_kk, rw_ka, rw_rk, rw_lnw, rw_lnb, w_ffn_gate, w_ffn_up, w_ffn_down, w_ple, w_ple_gate):
    bf = lambda w: w.astype(BF16)
    w_cat = jnp.concatenate([w_in_even[:, :, 0:1536], w_in_even[:, :, 1552:3600]], axis=-1)
    w_gk1 = jnp.pad(w_in_even[:, :, 1536:1552], ((0, 0), (0, 0), (0, LANE - GK_RANK)))
    w_gk2p = jnp.pad(w_gk2, ((0, 0), (0, LANE - GK_RANK), (0, 0)))
    pad_c = lambda w: jnp.pad(w, ((0, 0), (0, 0), (0, LANE - w.shape[2])))
    pad_r = lambda w: jnp.pad(w, ((0, 0), (0, LANE - w.shape[1]), (0, 0)))
    vec = lambda w: w.reshape(w.shape[0], 1, -1)
    wt = dict(
        norm_mix=vec(norm_mix), norm_ffn=vec(norm_ffn), norm_ple=vec(norm_ple),
        norm_final=norm_final.reshape(1, -1),
        w_cat=bf(w_cat), w_gk1=bf(w_gk1), w_gk2=bf(w_gk2p), b_gk=vec(b_gk),
        gla_norm=vec(gla_norm), hgrn_gamma=hgrn_gamma, hgrn_norm=vec(hgrn_norm),
        w_out_even=bf(w_out_even),
        rw_mu=rw_mu, rw_wr=bf(rw_wr), rw_wk=bf(rw_wk), rw_wv=bf(rw_wv), rw_wo=bf(rw_wo),
        rw_w0=vec(rw_w0), rw_w1=bf(pad_c(rw_w1)), rw_w2=bf(pad_r(rw_w2)),
        rw_a0=vec(rw_a0), rw_a1=bf(pad_c(rw_a1)), rw_a2=bf(pad_r(rw_a2)),
        rw_v0=vec(rw_v0), rw_v1=bf(pad_c(rw_v1)), rw_v2=bf(pad_r(rw_v2)),
        rw_g1=bf(rw_g1), rw_g2=bf(rw_g2), rw_kk=vec(rw_kk), rw_ka=vec(rw_ka), rw_rk=vec(rw_rk),
        rw_lnw=vec(rw_lnw), rw_lnb=vec(rw_lnb),
        w_ffn_gate=bf(w_ffn_gate), w_ffn_up=bf(w_ffn_up), w_ffn_down=bf(w_ffn_down),
        w_ple=bf(w_ple), w_ple_gate=bf(w_ple_gate),
    )
    bp = x_prompt.shape[0]
    zeros = lambda s: jnp.zeros((s.shape[0], bp) + s.shape[2:], F32)
    y_p, gla_p, hgrn_p, rwkv_p, shift_p = _run_group(
        x_prompt, p_prompt, zeros(state_gla), zeros(state_hgrn), zeros(state_rwkv),
        zeros(state_shift), wt)
    y_s, gla_s, hgrn_s, rwkv_s, shift_s = _run_group(
        x_sample, p_sample, state_gla, state_hgrn, state_rwkv, state_shift, wt)
    return (y_p, y_s, gla_p, hgrn_p, rwkv_p, shift_p, gla_s, hgrn_s, rwkv_s, shift_s)
```

```python
import functools
import math

import numpy as np
import jax
import jax.numpy as jnp
from jax import lax
from jax.experimental import pallas as pl
from jax.experimental.pallas import tpu as pltpu

F32 = jnp.float32
BF16 = jnp.bfloat16

D_MODEL = 1024
DEPTH = 4
MIX_A = 512
H_A = 4
DV_A = 128
DK_A = 64
GK_RANK = 16
GK_NORM = 16.0
H_B = 4
DV_B = 128
N_B = 128
LB_FLOOR = 1e-20
N_C = 64
H_C = D_MODEL // N_C
FFN_DIM = 2816
PLE_DIM = 256
RMS_EPS = 1e-6
GN_EPS = 64e-5
KK_EPS = 1e-24

LANE = 128
CHUNK = 64
CHUNKS_PER_STEP = 8
SUB = 16
VMEM_LIMIT = 56 * 1024 * 1024
FFN_SPLIT = 1408
PAIR_V = 2 * DV_A
SEG = 256
SEQ_ROWS = 16

EVEN_WIDTHS = (256, 256, 512, 512, 256, 512, 512, 512, 512, 512)
EVEN_DTYPES = (BF16, BF16, BF16, BF16, F32, BF16, BF16, BF16, BF16, F32)
RWKV_DTYPES = (BF16, F32, BF16, BF16, BF16, BF16, BF16)


def _cparams(sem):
    return pltpu.CompilerParams(dimension_semantics=sem, vmem_limit_bytes=VMEM_LIMIT)


def _const_spec(shape):
    nd = len(shape)
    return pl.BlockSpec(shape, lambda *_: (0,) * nd, pipeline_mode=pl.Buffered(1))


def _pspec(p):
    if not isinstance(p, tuple):
        return _const_spec(p.shape)
    a, layer = p
    nd = a.ndim - 1
    return pl.BlockSpec((None,) + a.shape[1:], lambda *_: (layer,) + (0,) * nd,
                        pipeline_mode=pl.Buffered(1))


def _parg(p):
    return p[0] if isinstance(p, tuple) else p


def _rms(x, g, eps=RMS_EPS):
    return x * lax.rsqrt(jnp.mean(x * x, axis=-1, keepdims=True) + eps) * g


def _sigmoid(x):
    return 1.0 / (1.0 + jnp.exp(-x))


def _silu(x):
    return x * _sigmoid(x)


def _log_sigmoid(x):
    return jnp.minimum(x, 0.0) - jnp.log1p(jnp.exp(-jnp.abs(x)))


def _dot(a, b):
    return jnp.dot(a, b, preferred_element_type=F32)


def _dot_nt(a, b):
    return lax.dot_general(a, b, (((1,), (1,)), ((), ())), preferred_element_type=F32)


def _dot_tn(a, b):
    return lax.dot_general(a, b, (((0,), (0,)), ((), ())), preferred_element_type=F32)


def _mm(a, b):
    return _dot(a.astype(BF16), b.astype(BF16))


def _split3(x):
    hi = x.astype(BF16)
    r1 = x - hi.astype(F32)
    mid = r1.astype(BF16)
    return hi, mid, (r1 - mid.astype(F32)).astype(BF16)


def _sum01_left(m01, x, terms=3):
    hi, mid, lo = _split3(x)
    acc = _dot(m01, hi) + _dot(m01, mid)
    return acc + _dot(m01, lo) if terms == 3 else acc


def _sum01_right(x, m01):
    n = x.shape[0]
    hi = x.astype(BF16)
    lo = (x - hi.astype(F32)).astype(BF16)
    y = _dot(jnp.concatenate([hi, lo], axis=0), m01)
    return y[0:n] + y[n:2 * n]


def _iota2(shape, axis):
    return lax.broadcasted_iota(jnp.int32, shape, axis)


def _stack_heads(x, first):
    return jnp.concatenate([jnp.where(first, x, 0), jnp.where(first, 0, x)], axis=0)


def _lockstep(gens):
    gens = list(gens)
    while gens:
        alive = []
        for g in gens:
            try:
                next(g)
                alive.append(g)
            except StopIteration:
                pass
        gens = alive


def _load_rows(ref, scr, c):
    t = ref.shape[1]
    if t == c:
        return ref[0]
    scr[...] = jnp.zeros(scr.shape, F32)
    scr[0:t, :] = ref[0].astype(F32)
    return scr[...]


def _put(ref, val):
    ref[...] = val.astype(ref.dtype)


def _even_pre_kernel(h_ref, gm_ref, w_ref, wgk1_ref, wgk2_ref, bgk_ref, gam_ref,
                     qa_ref, ka_ref, va_ref, ga_ref, lga_ref,
                     qb_ref, kb_ref, vb_ref, gb_ref, lgb_ref, *, layer):
    xn = _rms(h_ref[...], gm_ref[...]).astype(BF16)

    def proj(lo, hi):
        return _dot(xn, w_ref[:, lo:hi])

    _put(qa_ref, proj(0, 256) * DK_A ** -0.5)
    _put(ka_ref, proj(256, 512))
    _put(va_ref, proj(512, 1024))
    _put(ga_ref, _silu(proj(1024, 1536)))
    gk_lr = _dot(xn, wgk1_ref[...])
    gk = _mm(gk_lr, wgk2_ref[...]) + bgk_ref[...]
    _put(lga_ref, _log_sigmoid(gk) / GK_NORM)

    gam = gam_ref[...]
    n_even = gam.shape[0]
    gmax = gam[0:1]
    for i in range(1, n_even):
        gmax = jnp.maximum(gmax, gam[i:i + 1])
    es = [jnp.exp(gam[i:i + 1] - gmax) for i in range(n_even)]
    den = es[0]
    for i in range(1, n_even):
        den = den + es[i]
    sm = [e / den for e in es]
    cum = sm[0]
    for i in range(1, layer + 1):
        cum = cum + sm[i]
    lb = cum - sm[0]

    _put(qb_ref, _silu(proj(1536, 2048)) * N_B ** -0.5)
    z = proj(2048, 2560)
    _put(lgb_ref, jnp.log(jnp.maximum(lb, LB_FLOOR) + (1.0 - lb) * _sigmoid(z)))
    _put(kb_ref, (1.0 - lb) * _sigmoid(-z))
    _put(vb_ref, proj(2560, 3072))
    _put(gb_ref, _silu(proj(3072, 3584)))


def _even_pre(h, params, layer, tm):
    m = h.shape[0]
    row = lambda w: pl.BlockSpec((tm, w), lambda i: (i, 0))
    return pl.pallas_call(
        functools.partial(_even_pre_kernel, layer=layer),
        out_shape=[jax.ShapeDtypeStruct((m, w), d) for w, d in zip(EVEN_WIDTHS, EVEN_DTYPES)],
        grid=(m // tm,),
        in_specs=[row(D_MODEL)] + [_pspec(p) for p in params],
        out_specs=[row(w) for w in EVEN_WIDTHS],
        compiler_params=_cparams(("parallel",)),
        name="even_pre",
    )(h, *[_parg(p) for p in params])


def _level_widths(c):
    return [c >> i for i in range(1, int(math.log2(c)) + 1)]


def _decay_sum_matrix(c):
    i = np.arange(c)[:, None]
    t = np.arange(c)[None, :]
    blocks = [t <= i, t > i]
    for w in _level_widths(c):
        ref = (i // (2 * w)) * (2 * w) + w
        blocks.append(np.where(i >= ref, (t > ref) & (t <= i), (t > i) & (t <= ref)))
    return np.concatenate(blocks, axis=0).astype(np.float32)


def _level_index(c):
    t = np.arange(c)[:, None]
    s = np.arange(c)[None, :]
    lev = np.full((c, c), -1, np.int32)
    lev[t == s] = 0
    for li, w in enumerate(_level_widths(c)):
        m = (t // (2 * w) == s // (2 * w)) & (t % (2 * w) >= w) & (s % (2 * w) < w)
        lev[m] = li + 1
    return np.tile(lev, (1, 2))


def _row_to_columns(row, ones_rows):
    x = jnp.broadcast_to(row, (ones_rows.shape[0], row.shape[1]))
    hi = x.astype(BF16).astype(F32)
    r1 = x - hi
    mid = r1.astype(BF16).astype(F32)
    lo = (r1 - mid).astype(BF16).astype(F32)
    ridx = _iota2(x.shape, 0)
    terms = jnp.where(ridx == 0, hi, jnp.where(ridx == 1, mid, jnp.where(ridx == 2, lo, 0.0)))
    return _dot_tn(terms.astype(BF16), ones_rows)


def _gated_pair(q, k, v, e_all, lev, ones_rows, y_ref, out, c):
    kw = q.shape[1]
    nlev = len(_level_widths(c))
    first_k = _iota2((1, kw), 1) < kw // 2
    first_v = _iota2((1, PAIR_V), 1) < PAIR_V // 2
    parts = []
    for l in range(nlev + 1):
        if l == 0:
            q_l, k_l = q, k
        else:
            e = e_all[(l + 1) * c:(l + 2) * c]
            q_l, k_l = q * e, k * e
        parts.append(_dot_nt(q_l.astype(BF16), _stack_heads(k_l.astype(BF16), first_k)))
    y0 = y_ref[...]
    o_inter = _mm(q * e_all[0:c], y0)
    upd = _dot_tn((k * e_all[c:2 * c]).astype(BF16), v.astype(BF16))
    decay = _row_to_columns(e_all[c - 1:c], ones_rows)
    yield
    a_mat = jnp.where(lev == 0, parts[0], 0.0)
    for l in range(1, nlev + 1):
        a_mat = jnp.where(lev == l, parts[l], a_mat)
    o = o_inter + _dot(a_mat.astype(BF16), _stack_heads(v.astype(BF16), first_v))
    same_head = (_iota2((kw, 1), 0) // (kw // 2)) == (_iota2((1, PAIR_V), 1) // (PAIR_V // 2))
    y_ref[...] = decay * y0 + jnp.where(same_head, upd, 0.0)
    out.append(o)


def _even_mix_kernel(sg_acc_ref, sh_acc_ref, qa_ref, ka_ref, va_ref, ga_ref, lga_ref, qb_ref,
                     kb_ref, vb_ref, gb_ref, lgb_ref, msum_ref, lev_ref, ones_ref, sg0_ref,
                     sh0_ref, gn_ref, hn_ref, o_ref, sg_ref, sh_ref, ya_scr, yb_scr, *pads, c):
    del sg_acc_ref, sh_acc_ref
    step = pl.program_id(1)
    ka_w = 2 * DK_A
    kb_w = 2 * N_B

    @pl.when(step == 0)
    def _():
        ya_scr[...] = jnp.zeros(ya_scr.shape, F32)
        yb_scr[...] = jnp.zeros(yb_scr.shape, F32)
        for p in range(H_A // 2):
            ya_scr[p, 0:DK_A, 0:DV_A] = sg0_ref[0, 2 * p]
            ya_scr[p, DK_A:ka_w, DV_A:PAIR_V] = sg0_ref[0, 2 * p + 1]
        for p in range(H_B // 2):
            yb_scr[p, 0:N_B, 0:DV_B] = sh0_ref[0, 2 * p]
            yb_scr[p, N_B:kb_w, DV_B:PAIR_V] = sh0_ref[0, 2 * p + 1]

    ins = (qa_ref, ka_ref, va_ref, ga_ref, lga_ref, qb_ref, kb_ref, vb_ref, gb_ref, lgb_ref)
    msum = msum_ref[...]
    lev = lev_ref[...]
    ones_cv = ones_ref[...]
    rows = o_ref.shape[1]
    for r0 in range(0, rows, c):
        if pads:
            vals = [_load_rows(r, s, c) for r, s in zip(ins, pads)]
        else:
            vals = [r[0, r0:r0 + c, :].astype(F32) for r in ins]
        qa, ka, va, ga, lga, qb, kb, vb, gb, lgb = vals
        t_out = min(c, rows)
        ea = jnp.exp(_sum01_left(msum, lga))
        eb = jnp.exp(_sum01_left(msum, lgb))

        outs_a = [[] for _ in range(H_A // 2)]
        outs_b = [[] for _ in range(H_B // 2)]
        gens = []
        for p in range(H_A // 2):
            ks = slice(p * ka_w, (p + 1) * ka_w)
            vs = slice(p * PAIR_V, (p + 1) * PAIR_V)
            gens.append(_gated_pair(qa[:, ks], ka[:, ks], va[:, vs], ea[:, ks], lev,
                                    ones_cv, ya_scr.at[p], outs_a[p], c))
        for p in range(H_B // 2):
            ks = slice(p * kb_w, (p + 1) * kb_w)
            vs = slice(p * PAIR_V, (p + 1) * PAIR_V)
            gens.append(_gated_pair(qb[:, ks], kb[:, ks], vb[:, vs], eb[:, ks], lev,
                                    ones_cv, yb_scr.at[p], outs_b[p], c))
        _lockstep(gens)

        for h in range(H_A):
            o = outs_a[h // 2][0][:, (h % 2) * DV_A:(h % 2 + 1) * DV_A]
            vs = slice(h * DV_A, (h + 1) * DV_A)
            res = _rms(o, gn_ref[...]) * ga[:, vs]
            o_ref[0, r0:r0 + t_out, vs] = res[0:t_out].astype(o_ref.dtype)
        for h in range(H_B):
            o = outs_b[h // 2][0][:, (h % 2) * DV_B:(h % 2 + 1) * DV_B]
            vs = slice(MIX_A + h * DV_B, MIX_A + (h + 1) * DV_B)
            res = _rms(o, hn_ref[...]) * gb[:, h * DV_B:(h + 1) * DV_B]
            o_ref[0, r0:r0 + t_out, vs] = res[0:t_out].astype(o_ref.dtype)

    @pl.when(step == pl.num_programs(1) - 1)
    def _():
        for p in range(H_A // 2):
            sg_ref[0, 2 * p] = ya_scr[p, 0:DK_A, 0:DV_A]
            sg_ref[0, 2 * p + 1] = ya_scr[p, DK_A:ka_w, DV_A:PAIR_V]
        for p in range(H_B // 2):
            sh_ref[0, 2 * p] = yb_scr[p, 0:N_B, 0:DV_B]
            sh_ref[0, 2 * p + 1] = yb_scr[p, N_B:kb_w, DV_B:PAIR_V]


def _layer_state_spec(shape, layer, grid_rank):
    tail = (0,) * (len(shape) - 2)
    if grid_rank == 2:
        return pl.BlockSpec((None, 1) + shape[2:], lambda b, s: (layer, b) + tail)
    return pl.BlockSpec((None, 1) + shape[2:], lambda b: (layer, b) + tail)


_ANY_SPEC = pl.BlockSpec(memory_space=pl.ANY)


def _even_mix(pre, sg_in, sh_in, sg_acc, sh_acc, layer, gn, hn, bsz, t_len):
    if t_len % CHUNK == 0:
        c, tb = CHUNK, CHUNK * CHUNKS_PER_STEP
        assert t_len % tb == 0
    else:
        c, tb = SEQ_ROWS, t_len
        assert t_len < c
    pre = [a.reshape(bsz, t_len, a.shape[-1]) for a in pre]
    msum = jnp.asarray(_decay_sum_matrix(c), BF16)
    lev = jnp.asarray(_level_index(c))
    ones_cv = jnp.ones((SEQ_ROWS, PAIR_V), BF16)
    row = lambda w: pl.BlockSpec((1, tb, w), lambda b, s: (b, s, 0))
    st = lambda a: _layer_state_spec(a.shape, layer, 2)
    scratch = [pltpu.VMEM((H_A // 2, 2 * DK_A, PAIR_V), F32),
               pltpu.VMEM((H_B // 2, 2 * N_B, PAIR_V), F32)]
    if tb < c:
        scratch += [pltpu.VMEM((c, w), F32) for w in EVEN_WIDTHS]
    o_dtype = BF16 if tb >= c else F32
    o, sg, sh = pl.pallas_call(
        functools.partial(_even_mix_kernel, c=c),
        out_shape=[jax.ShapeDtypeStruct((bsz, t_len, D_MODEL), o_dtype),
                   jax.ShapeDtypeStruct(sg_acc.shape, F32),
                   jax.ShapeDtypeStruct(sh_acc.shape, F32)],
        grid=(bsz, t_len // tb),
        in_specs=[_ANY_SPEC, _ANY_SPEC] + [row(w) for w in EVEN_WIDTHS] + [
            _const_spec(msum.shape), _const_spec(lev.shape), _const_spec(ones_cv.shape),
            st(sg_in), st(sh_in), _pspec(gn), _pspec(hn)],
        out_specs=[row(D_MODEL), st(sg_acc), st(sh_acc)],
        scratch_shapes=scratch,
        input_output_aliases={0: 1, 1: 2},
        compiler_params=_cparams(("parallel", "arbitrary")),
        name="even_mix",
    )(sg_acc, sh_acc, *pre, msum, lev, ones_cv, sg_in, sh_in, _parg(gn), _parg(hn))
    return o.reshape(bsz * t_len, D_MODEL), sg, sh


def _rwkv_pre_kernel(*refs, t_len, has_vmix):
    (h_ref, xl_ref, gm_ref, mu_ref, wr_ref, wk_ref, wv_ref, w0_ref, w1_ref, w2_ref,
     a0_ref, a1_ref, a2_ref, g1_ref, g2_ref, kk_ref, ka_ref) = refs[:17]
    pos = 17
    if has_vmix:
        vf_ref, v0_ref, v1_ref, v2_ref = refs[pos:pos + 4]
        pos += 4
    (r_out, lw_out, k_out, v_out, kkr_out, a_out, g_out, xl_out, carry, stage) = refs[pos:]

    tm = h_ref.shape[0]
    step = pl.program_id(0)
    xn = _rms(h_ref[...], gm_ref[...])
    rows = _iota2((tm, 1), 0)
    prev = pltpu.roll(xn, 1, 0)
    if t_len % tm == 0:
        seq = (step * tm) // t_len
        edge = jnp.where((step * tm) % t_len == 0, xl_ref[pl.ds(seq, 1), :], carry[0:1, :])
        prev = jnp.where(rows == 0, edge, prev)
        carry[0:1, :] = xn[tm - 1:tm, :]

        @pl.when((step * tm + tm) % t_len == 0)
        def _():
            xl_out[pl.ds(seq, 1), :] = xn[tm - 1:tm, :]
    else:
        n_seq = tm // t_len
        first_rows = pl.ds(0, n_seq, stride=t_len)
        last_rows = pl.ds(t_len - 1, n_seq, stride=t_len)
        seqs = pl.ds(step * n_seq, n_seq)
        stage[...] = jnp.zeros(stage.shape, F32)
        for j in range(stage.shape[0]):
            stage[j, first_rows, :] = xl_ref[seqs, j * LANE:(j + 1) * LANE]
        spread = jnp.concatenate([stage[j] for j in range(stage.shape[0])], axis=1)
        prev = jnp.where(rows % t_len == 0, spread, prev)
        for j in range(stage.shape[0]):
            stage[j] = xn[:, j * LANE:(j + 1) * LANE]
            xl_out[seqs, j * LANE:(j + 1) * LANE] = stage[j, last_rows, :]
    dx = prev - xn

    def mixed(i):
        return (xn + dx * mu_ref[i:i + 1, :]).astype(BF16)

    _put(r_out, _dot(mixed(0), wr_ref[...]))
    dec = w0_ref[...] + _mm(jnp.tanh(_dot(mixed(1), w1_ref[...])), w2_ref[...])
    _put(lw_out, -_sigmoid(dec) * math.exp(-0.5))
    k = _dot(mixed(2), wk_ref[...])
    xv = mixed(3)
    v = _dot(xv, wv_ref[...])
    a = _sigmoid(a0_ref[...] + _mm(_dot(mixed(4), a1_ref[...]), a2_ref[...]))
    _put(a_out, a)
    _put(g_out, _mm(_sigmoid(_dot(mixed(5), g1_ref[...])), g2_ref[...]))
    _put(kkr_out, k * kk_ref[...])
    _put(k_out, k * (1.0 + (a - 1.0) * ka_ref[...]))
    if has_vmix:
        gate = _sigmoid(v0_ref[...] + _mm(_dot(xv, v1_ref[...]), v2_ref[...]))
        v = v + (vf_ref[...].astype(F32) - v) * gate
    _put(v_out, v)


def _rwkv_pre(h, xl, t_len, wts, vmix, tm):
    m = h.shape[0]
    assert t_len % tm == 0 or tm % t_len == 0
    row = pl.BlockSpec((tm, D_MODEL), lambda i: (i, 0))
    args = [h, xl] + [_parg(w) for w in wts]
    specs = [row, _const_spec(xl.shape)] + [_pspec(w) for w in wts]
    if vmix is not None:
        args += [vmix[0]] + [_parg(w) for w in vmix[1:]]
        specs += [row] + [_pspec(w) for w in vmix[1:]]
    return pl.pallas_call(
        functools.partial(_rwkv_pre_kernel, t_len=t_len, has_vmix=vmix is not None),
        out_shape=[jax.ShapeDtypeStruct((m, D_MODEL), d) for d in RWKV_DTYPES]
        + [jax.ShapeDtypeStruct(xl.shape, F32)],
        grid=(m // tm,),
        in_specs=specs,
        out_specs=[row] * len(RWKV_DTYPES) + [pl.BlockSpec(xl.shape, lambda i: (0, 0))],
        scratch_shapes=[pltpu.VMEM((8, D_MODEL), F32),
                        pltpu.VMEM((D_MODEL // LANE, tm, LANE), F32)],
        compiler_params=_cparams(("arbitrary",)),
        name="rwkv_pre",
    )(*args)


def _mm_pair(x, y, first, x_terms=2, y_terms=2):
    m = x.shape[0]
    xh = x.astype(BF16)
    yh = y.astype(BF16)
    ybh = _stack_heads(yh, first)
    if x_terms == 2:
        both = _dot(jnp.concatenate([xh, (x - xh.astype(F32)).astype(BF16)], axis=0), ybh)
        acc = both[0:m] + both[m:2 * m]
    else:
        acc = _dot(xh, ybh)
    if y_terms == 2:
        acc = acc + _dot(xh, _stack_heads((y - yh.astype(F32)).astype(BF16), first))
    return acc


def _segsum(x, seg):
    c = x.shape[0]
    n = D_MODEL // SEG
    xs = jnp.concatenate([x[:, i * SEG:(i + 1) * SEG] for i in range(n)], axis=0)
    ys = _sum01_right(xs, seg)
    return jnp.concatenate([ys[i * c:(i + 1) * c] for i in range(n)], axis=1)


def _rwkv_head_out(o, r, k, v, g, rk_ref, lnw_ref, lnb_ref, seg):
    mean = _segsum(o, seg) * (1.0 / N_C)
    var = _segsum(jnp.square(o - mean), seg) * (1.0 / N_C)
    on = (o - mean) * lax.rsqrt(var + GN_EPS) * lnw_ref[...] + lnb_ref[...]
    bonus = _segsum(r * k * rk_ref[...], seg) * v
    return (on + bonus) * g


def _rwkv_pair(a_t, r_t, b_t, k_t, b_e, k_e, v, w_tot, y_ref, out, c):
    t_i = _iota2((c, LANE), 0)
    s_i = _iota2((c, LANE), 1) & (c - 1)
    first = _iota2((1, LANE), 1) < N_C
    strict = s_i < t_i
    incl = s_i <= t_i
    eye = (s_i == t_i).astype(F32)
    same_blk = (s_i // SUB) == (t_i // SUB)

    lhs = jnp.concatenate([a_t, r_t], axis=0).astype(BF16)
    rhs = jnp.concatenate([_stack_heads(b_t.astype(BF16), first),
                           _stack_heads(k_t.astype(BF16), first)], axis=0)
    m1 = _dot_nt(lhs, rhs)
    l_ab = jnp.where(strict, m1[0:c, 0:LANE], 0.0).astype(BF16).astype(F32)
    l_ak = jnp.where(strict, m1[0:c, LANE:2 * LANE], 0.0)
    m_rb = jnp.where(incl, m1[c:2 * c, 0:LANE], 0.0)
    m_rk = jnp.where(incl, m1[c:2 * c, LANE:2 * LANE], 0.0)
    y0 = y_ref[...]
    xs = _dot_nt(lhs, y0.astype(BF16))
    v_bd = _stack_heads(v.astype(BF16), first)
    rhs0 = xs[0:c] + _dot(l_ak.astype(BF16), v_bd)
    yield
    l_bd = jnp.where(same_blk, l_ab, 0.0)
    l_off = l_ab - l_bd
    l2 = _mm_pair(l_bd, l_bd, first, 1, 1)
    p = eye - l_bd
    yield
    both = _mm_pair(jnp.concatenate([p, l2], axis=0), l2, first)
    p = p + both[0:c]
    l4 = both[c:2 * c]
    yield
    both = _mm_pair(jnp.concatenate([p, l4], axis=0), l4, first)
    p = p + both[0:c]
    l8 = both[c:2 * c]
    yield
    p = p + _mm_pair(p, l8, first)
    yield
    n = _mm_pair(p, l_off, first, 1, 1)
    yield
    n2 = _mm_pair(n, n, first, 1, 1)
    q = eye - n
    yield
    q = q + _mm_pair(q, n2, first, 1, 1)
    yield
    w = _mm_pair(p, rhs0, first, 1, 1)
    yield
    u = -_mm_pair(q, w, first, 1, 1)
    yield
    mr = jnp.concatenate([m_rb, m_rk], axis=1).astype(BF16)
    uv_bd = jnp.concatenate([_stack_heads(u.astype(BF16), first), v_bd], axis=0)
    out.append(xs[c:2 * c] + _dot(mr, uv_bd))
    uv = jnp.concatenate([u, v], axis=0).astype(BF16)
    bk = jnp.concatenate([b_e, k_e], axis=0).astype(BF16)
    same_head = (_iota2((LANE, 1), 0) // N_C) == (_iota2((1, LANE), 1) // N_C)
    y_ref[...] = y0 * jnp.exp(w_tot) + jnp.where(same_head, _dot_tn(uv, bk), 0.0)


def _rwkv_mix_kernel(s_acc_ref, r_ref, lw_ref, k_ref, v_ref, kkr_ref, a_ref, g_ref, s0_ref, rk_ref,
                     lnw_ref, lnb_ref, tri_ref, seg_ref, o_ref, s_out_ref, y_scr, *, c):
    del s_acc_ref
    step = pl.program_id(1)
    n_pair = H_C // 2

    @pl.when(step == 0)
    def _():
        y_scr[...] = jnp.zeros(y_scr.shape, F32)
        for p in range(n_pair):
            y_scr[p, 0:N_C, 0:N_C] = s0_ref[0, 2 * p]
            y_scr[p, N_C:LANE, N_C:LANE] = s0_ref[0, 2 * p + 1]

    ins = (r_ref, lw_ref, k_ref, v_ref, kkr_ref, a_ref, g_ref)
    seg = seg_ref[...]
    for r0 in range(0, o_ref.shape[1], c):
        r, lw, k, v, kk, a, g = [x[0, r0:r0 + c, :].astype(F32) for x in ins]
        kk = kk * lax.rsqrt(jnp.maximum(_segsum(kk * kk, seg), KK_EPS))
        cw = _sum01_left(tri_ref[...], lw, terms=2)
        w_tot = cw[c - 1:c]
        kka = kk * a
        a_t = kk * jnp.exp(cw - lw)
        r_t = r * jnp.exp(cw)
        e_inv = jnp.exp(-cw)
        b_t = kka * e_inv
        k_t = k * e_inv
        e_end = jnp.exp(w_tot - cw)
        b_e = kka * e_end
        k_e = k * e_end

        outs = [[] for _ in range(n_pair)]
        gens = []
        for p in range(n_pair):
            sl = slice(p * LANE, (p + 1) * LANE)
            gens.append(_rwkv_pair(a_t[:, sl], r_t[:, sl], b_t[:, sl], k_t[:, sl], b_e[:, sl],
                                   k_e[:, sl], v[:, sl], w_tot[:, sl], y_scr.at[p], outs[p], c))
        _lockstep(gens)
        o = jnp.concatenate([outs[p][0] for p in range(n_pair)], axis=1)
        res = _rwkv_head_out(o, r, k, v, g, rk_ref, lnw_ref, lnb_ref, seg)
        o_ref[0, r0:r0 + c, :] = res.astype(o_ref.dtype)

    @pl.when(step == pl.num_programs(1) - 1)
    def _():
        for p in range(n_pair):
            s_out_ref[0, 2 * p] = y_scr[p, 0:N_C, 0:N_C]
            s_out_ref[0, 2 * p + 1] = y_scr[p, N_C:LANE, N_C:LANE]


def _rwkv_mix(pre, s_in, s_acc, layer, rk, lnw, lnb, bsz, t_len):
    c = CHUNK
    tb = c * CHUNKS_PER_STEP
    assert 2 * c == LANE and c // SUB == 4 and t_len % tb == 0
    pre = [a.reshape(bsz, t_len, D_MODEL) for a in pre]
    tri = jnp.asarray(np.tril(np.ones((c, c), np.float32)), BF16)
    seg = jnp.asarray(np.kron(np.eye(SEG // N_C, dtype=np.float32),
                              np.ones((N_C, N_C), np.float32)), BF16)
    row = pl.BlockSpec((1, tb, D_MODEL), lambda b, s: (b, s, 0))
    st = _layer_state_spec(s_in.shape, layer, 2)
    scratch = [pltpu.VMEM((H_C // 2, LANE, LANE), F32)]
    o, s_new = pl.pallas_call(
        functools.partial(_rwkv_mix_kernel, c=c),
        out_shape=[jax.ShapeDtypeStruct((bsz, t_len, D_MODEL), BF16),
                   jax.ShapeDtypeStruct(s_acc.shape, F32)],
        grid=(bsz, t_len // tb),
        in_specs=[_ANY_SPEC] + [row] * 7 + [st, _pspec(rk), _pspec(lnw), _pspec(lnb),
                                            _const_spec(tri.shape), _const_spec(seg.shape)],
        out_specs=[row, st],
        scratch_shapes=scratch,
        input_output_aliases={0: 1},
        compiler_params=_cparams(("parallel", "arbitrary")),
        name="rwkv_mix",
    )(s_acc, *pre, s_in, _parg(rk), _parg(lnw), _parg(lnb), tri, seg)
    return o.reshape(bsz * t_len, D_MODEL), s_new


def _rwkv_lane_kernel(s_acc_ref, r_ref, lw_ref, k_ref, v_ref, kkr_ref, a_ref, g_ref, s0_ref, rk_ref,
                      lnw_ref, lnb_ref, o_ref, s_out_ref, v_scr, o_scr):
    del s_acc_ref
    t_len = r_ref.shape[0]
    for t in range(t_len):
        kk = kkr_ref[t].astype(F32)
        kk = kk * lax.rsqrt(jnp.maximum(jnp.sum(kk * kk, axis=0, keepdims=True), KK_EPS))
        kka = kk * a_ref[t].astype(F32)
        w = jnp.exp(lw_ref[t])
        k = k_ref[t].astype(F32)
        r = r_ref[t].astype(F32)
        v = v_ref[t].astype(F32)
        v_scr[...] = v
        src = s0_ref if t == 0 else s_out_ref

        def value_row(i, carry, src=src, kk=kk, kka=kka, w=w, k=k, r=r):
            s_i = src[i]
            s_kk = jnp.sum(s_i * kk, axis=0, keepdims=True)
            s_i = s_i * w - s_kk * kka + v_scr[pl.ds(i, 1), :] * k
            s_out_ref[i] = s_i
            o_scr[pl.ds(i, 1), :] = jnp.sum(s_i * r, axis=0, keepdims=True)
            return carry

        lax.fori_loop(0, N_C, value_row, 0, unroll=4)
        o = o_scr[...]
        mean = jnp.mean(o, axis=0, keepdims=True)
        var = jnp.mean(jnp.square(o - mean), axis=0, keepdims=True)
        on = (o - mean) * lax.rsqrt(var + GN_EPS) * lnw_ref[...] + lnb_ref[...]
        bonus = jnp.sum(r * k * rk_ref[...], axis=0, keepdims=True) * v
        o_ref[t] = ((on + bonus) * g_ref[t].astype(F32)).astype(o_ref.dtype)


def _rwkv_lane(pre, s_in, s_acc, layer, rk, lnw, lnb, bsz, t_len):
    assert bsz % LANE == 0 and t_len < CHUNK
    to_lanes = lambda a: a.reshape(bsz, t_len, D_MODEL).transpose(1, 2, 0)
    pre = [to_lanes(a) for a in pre]
    chan = lambda p: jnp.broadcast_to(p[0][p[1]].reshape(D_MODEL, 1), (D_MODEL, bsz))
    tok = pl.BlockSpec((t_len, N_C, bsz), lambda h: (0, h, 0))
    par = pl.BlockSpec((N_C, bsz), lambda h: (h, 0))
    st = pl.BlockSpec((None, None) + s_in.shape[2:], lambda h: (layer, h, 0, 0, 0))
    o, s_new = pl.pallas_call(
        _rwkv_lane_kernel,
        out_shape=[jax.ShapeDtypeStruct((t_len, D_MODEL, bsz), BF16),
                   jax.ShapeDtypeStruct(s_acc.shape, F32)],
        grid=(H_C,),
        in_specs=[_ANY_SPEC] + [tok] * 7 + [st, par, par, par],
        out_specs=[tok, st],
        scratch_shapes=[pltpu.VMEM((N_C, bsz), F32), pltpu.VMEM((N_C, bsz), F32)],
        input_output_aliases={0: 1},
        compiler_params=_cparams(("parallel",)),
        name="rwkv_lane",
    )(s_acc, *pre, s_in, chan(rk), chan(lnw), chan(lnb))
    return o.transpose(2, 0, 1).reshape(bsz * t_len, D_MODEL), s_new


def _post_kernel(h_ref, o_ref, wo_ref, gf_ref, wg_ref, wu_ref, wd_ref, gp_ref, wpg_ref,
                 p_ref, wple_ref, gfin_ref, out_ref, *, final):
    h1 = h_ref[...] + _dot(o_ref[...].astype(BF16), wo_ref[...])
    x2 = _rms(h1, gf_ref[...]).astype(BF16)
    h2 = h1
    for lo in range(0, FFN_DIM, FFN_SPLIT):
        gate = _dot(x2, wg_ref[:, lo:lo + FFN_SPLIT])
        up = _dot(x2, wu_ref[:, lo:lo + FFN_SPLIT])
        act = (_silu(gate) * up).astype(BF16)
        h2 = h2 + _dot(act, wd_ref[lo:lo + FFN_SPLIT, :])
    x3 = _rms(h2, gp_ref[...]).astype(BF16)
    gate = _sigmoid(_dot(x3, wpg_ref[...]))
    emb = _dot(p_ref[...].astype(BF16), wple_ref[...])
    h3 = h2 + gate * emb
    out_ref[...] = _rms(h3, gfin_ref[...]) if final else h3


def _post(h, o, p, layer, wts, final, tm):
    m = h.shape[0]
    wo, gf, wg, wu, wd, gp, wpg, wple, gfin = wts
    row = lambda w: pl.BlockSpec((tm, w), lambda i: (i, 0))
    p_spec = pl.BlockSpec((None, tm, PLE_DIM), lambda i: (layer, i, 0))
    return pl.pallas_call(
        functools.partial(_post_kernel, final=final),
        out_shape=jax.ShapeDtypeStruct((m, D_MODEL), F32),
        grid=(m // tm,),
        in_specs=[row(D_MODEL), row(D_MODEL), _pspec(wo), _pspec(gf), _pspec(wg), _pspec(wu),
                  _pspec(wd), _pspec(gp), _pspec(wpg), p_spec, _pspec(wple), _pspec(gfin)],
        out_specs=row(D_MODEL),
        compiler_params=_cparams(("parallel",)),
        name="post",
    )(h, o, *[_parg(w) for w in (wo, gf, wg, wu, wd, gp, wpg)], p, _parg(wple), _parg(gfin))


def _run_group(x, p, s_gla, s_hgrn, s_rwkv, s_shift, wt):
    bsz, t_len, _ = x.shape
    m = bsz * t_len
    tm = min(512, m)
    h = x.reshape(m, D_MODEL)
    p = p.reshape(DEPTH, m, PLE_DIM)
    long_seq = t_len % CHUNK == 0
    if not long_seq:
        s_rwkv = s_rwkv.transpose(0, 2, 3, 4, 1)
    acc_gla = jnp.zeros(s_gla.shape, F32)
    acc_hgrn = jnp.zeros(s_hgrn.shape, F32)
    acc_rwkv = jnp.zeros(s_rwkv.shape, F32)
    out_shift = []
    v_first = None
    par = lambda name, layer: (wt[name], layer)
    for i in range(DEPTH):
        j = i // 2
        gm = par('norm_mix', i)
        if i % 2 == 0:
            pre = _even_pre(h, [gm, par('w_cat', j), par('w_gk1', j), par('w_gk2', j),
                                par('b_gk', j), wt['hgrn_gamma']], j, tm)
            o, acc_gla, acc_hgrn = _even_mix(pre, s_gla, s_hgrn, acc_gla, acc_hgrn, j,
                                             par('gla_norm', j), par('hgrn_norm', j), bsz, t_len)
            w_o = par('w_out_even', j)
        else:
            wts = [gm] + [par(n, j) for n in ('rw_mu', 'rw_wr', 'rw_wk', 'rw_wv', 'rw_w0', 'rw_w1',
                                               'rw_w2', 'rw_a0', 'rw_a1', 'rw_a2', 'rw_g1', 'rw_g2',
                                               'rw_kk', 'rw_ka')]
            vmix = None
            if j > 0:
                vmix = [v_first] + [par(n, j - 1) for n in ('rw_v0', 'rw_v1', 'rw_v2')]
            r, lw, k, v, kkr, a, g, x_last = _rwkv_pre(h, s_shift[j], t_len, wts, vmix, tm)
            if j == 0:
                v_first = v
            rwkv = _rwkv_mix if long_seq else _rwkv_lane
            o, acc_rwkv = rwkv((r, lw, k, v, kkr, a, g), s_rwkv, acc_rwkv, j, par('rw_rk', j),
                               par('rw_lnw', j), par('rw_lnb', j), bsz, t_len)
            out_shift.append(x_last)
            w_o = par('rw_wo', j)
        post_w = [w_o] + [par(n, i) for n in ('norm_ffn', 'w_ffn_gate', 'w_ffn_up', 'w_ffn_down',
                                              'norm_ple', 'w_ple_gate', 'w_ple')] + [wt['norm_final']]
        h = _post(h, o, p, i, post_w, i == DEPTH - 1, tm)
    if not long_seq:
        acc_rwkv = acc_rwkv.transpose(0, 4, 1, 2, 3)
    return (h.reshape(bsz, t_len, D_MODEL), acc_gla, acc_hgrn, acc_rwkv, jnp.stack(out_shift))


def kernel(x_prompt, x_sample, p_prompt, p_sample, state_gla, state_hgrn, state_rwkv, state_shift, norm_mix, norm_ffn, norm_ple, norm_final, w_in_even, w_gk2, b_gk, gla_norm, hgrn_gamma, hgrn_norm, w_out_even, rw_mu, rw_wr, rw_wk, rw_wv, rw_wo, rw_w0, rw_w1, rw_w2, rw_a0, rw_a1, rw_a2, rw_v0, rw_v1, rw_v2, rw_g1, rw_g2, rw_kk, rw_ka, rw_rk, rw_lnw, rw_lnb, w_ffn_gate, w_ffn_up, w_ffn_down, w_ple, w_ple_gate):
    bf = lambda w: w.astype(BF16)
    w_cat = jnp.concatenate([w_in_even[:, :, 0:1536], w_in_even[:, :, 1552:3600]], axis=-1)
    w_gk1 = jnp.pad(w_in_even[:, :, 1536:1552], ((0, 0), (0, 0), (0, LANE - GK_RANK)))
    w_gk2p = jnp.pad(w_gk2, ((0, 0), (0, LANE - GK_RANK), (0, 0)))
    pad_c = lambda w: jnp.pad(w, ((0, 0), (0, 0), (0, LANE - w.shape[2])))
    pad_r = lambda w: jnp.pad(w, ((0, 0), (0, LANE - w.shape[1]), (0, 0)))
    vec = lambda w: w.reshape(w.shape[0], 1, -1)
    wt = dict(
        norm_mix=vec(norm_mix), norm_ffn=vec(norm_ffn), norm_ple=vec(norm_ple),
        norm_final=norm_final.reshape(1, -1),
        w_cat=bf(w_cat), w_gk1=bf(w_gk1), w_gk2=bf(w_gk2p), b_gk=vec(b_gk),
        gla_norm=vec(gla_norm), hgrn_gamma=hgrn_gamma, hgrn_norm=vec(hgrn_norm),
        w_out_even=bf(w_out_even),
        rw_mu=rw_mu, rw_wr=bf(rw_wr), rw_wk=bf(rw_wk), rw_wv=bf(rw_wv), rw_wo=bf(rw_wo),
        rw_w0=vec(rw_w0), rw_w1=bf(pad_c(rw_w1)), rw_w2=bf(pad_r(rw_w2)),
        rw_a0=vec(rw_a0), rw_a1=bf(pad_c(rw_a1)), rw_a2=bf(pad_r(rw_a2)),
        rw_v0=vec(rw_v0), rw_v1=bf(pad_c(rw_v1)), rw_v2=bf(pad_r(rw_v2)),
        rw_g1=bf(rw_g1), rw_g2=bf(rw_g2), rw_kk=vec(rw_kk), rw_ka=vec(rw_ka), rw_rk=vec(rw_rk),
        rw_lnw=vec(rw_lnw), rw_lnb=vec(rw_lnb),
        w_ffn_gate=bf(w_ffn_gate), w_ffn_up=bf(w_ffn_up), w_ffn_down=bf(w_ffn_down),
        w_ple=bf(w_ple), w_ple_gate=bf(w_ple_gate),
    )
    bp = x_prompt.shape[0]
    zeros = lambda s: jnp.zeros((s.shape[0], bp) + s.shape[2:], F32)
    y_p, gla_p, hgrn_p, rwkv_p, shift_p = _run_group(
        x_prompt, p_prompt, zeros(state_gla), zeros(state_hgrn), zeros(state_rwkv),
        zeros(state_shift), wt)
    y_s, gla_s, hgrn_s, rwkv_s, shift_s = _run_group(
        x_sample, p_sample, state_gla, state_hgrn, state_rwkv, state_shift, wt)
    return (y_p, y_s, gla_p, hgrn_p, rwkv_p, shift_p, gla_s, hgrn_s, rwkv_s, shift_s)
```
